```python
import math
import jax, jax.numpy as jnp
from jax import lax
import numpy as np

D_MODEL = 1024
BATCH = 4
SEQ = 8192
DEPTH = 4

CHUNK = 64
Q_BLOCK = 128
N_HEADS = 4
DA_HEAD = 64
DA_VDIM = 2 * DA_HEAD
RET_DK = 64
RET_DV = 128
ROPE_BASE = 10000.0
GLA_DK = 64
GLA_DV = 128
GLA_RANK = 16
GLA_TAU = 16.0
D_FF = 2752
EPS = 1e-6
N_BRANCH = 3

A_Q = N_HEADS * 2 * DA_HEAD
A_K = N_HEADS * 2 * DA_HEAD
A_V = N_HEADS * DA_VDIM
B_Q = N_HEADS * RET_DK
B_K = N_HEADS * RET_DK
B_V = N_HEADS * RET_DV
B_G = N_HEADS * RET_DV
C_Q = N_HEADS * GLA_DK
C_K = N_HEADS * GLA_DK
C_V = N_HEADS * GLA_DV
C_G = N_HEADS * GLA_DV
C_A = GLA_RANK
GATE_COLS = N_BRANCH * D_MODEL
IN_COLS = A_Q + A_K + A_V + B_Q + B_K + B_V + B_G + C_Q + C_K + C_V + C_G + C_A + GATE_COLS

kernel_name = "hybrid_diffattn_retention_gla_macaron"


def _rmsnorm(t, g):
    tf = t.astype(jnp.float32)
    y = tf * lax.rsqrt(jnp.mean(tf * tf, axis=-1, keepdims=True) + EPS)
    return (y * g.astype(jnp.float32)).astype(t.dtype)


def _swiglu(t, w1, w3, w2):
    return (jax.nn.silu(t @ w1) * (t @ w3)) @ w2


def _heads(t, h):
    b, s, w = t.shape
    return t.reshape(b, s, h, w // h).transpose(0, 2, 1, 3)


def _merge_heads(t):
    b, h, s, d = t.shape
    return t.transpose(0, 2, 1, 3).reshape(b, s, h * d)


def _to_chunks(t):
    b, h, s, d = t.shape
    return t.reshape(b, h, s // CHUNK, CHUNK, d).transpose(2, 0, 1, 3, 4)


def _from_chunks(t):
    n, b, h, c, d = t.shape
    return t.transpose(1, 2, 0, 3, 4).reshape(b, h, n * c, d)


def _rotary(t):
    s, d = t.shape[-2], t.shape[-1]
    inv = ROPE_BASE ** (-jnp.arange(0, d, 2, dtype=jnp.float32) / d)
    ang = jnp.arange(s, dtype=jnp.float32)[:, None] * inv[None, :]
    cos, sin = jnp.cos(ang), jnp.sin(ang)
    t1, t2 = t[..., : d // 2], t[..., d // 2:]
    return jnp.concatenate([t1 * cos - t2 * sin, t1 * sin + t2 * cos], axis=-1)


def _diff_attention(q, k, v, lam):
    b, h, _, s, dh = q.shape
    nb = s // Q_BLOCK
    qb = q.reshape(b, h, 2, nb, Q_BLOCK, dh).transpose(3, 0, 1, 2, 4, 5)
    key_chunk = jnp.arange(s) // CHUNK
    scale = dh ** -0.5

    def block(args):
        qi, bi = args
        q_chunk = (bi * Q_BLOCK + jnp.arange(Q_BLOCK)) // CHUNK
        mask = key_chunk[None, :] <= q_chunk[:, None]
        sc = jnp.einsum('bhmqd,bhmkd->bhmqk', qi, k).astype(jnp.float32) * scale
        p = jax.nn.softmax(jnp.where(mask, sc, -jnp.inf), axis=-1)
        w = p[:, :, 0] - lam * p[:, :, 1]
        return jnp.einsum('bhqk,bhkd->bhqd', w.astype(v.dtype), v)

    out = lax.map(block, (qb, jnp.arange(nb)))
    return out.transpose(1, 2, 0, 3, 4).reshape(b, h, s, v.shape[-1])


def _retention(q, k, v):
    b, h, _, dk = q.shape
    log_g = jnp.log1p(-jnp.exp2(-5.0 - jnp.arange(h, dtype=jnp.float32)))
    pos = jnp.arange(CHUNK, dtype=jnp.float32)
    causal = pos[:, None] >= pos[None, :]
    d_in = jnp.exp(jnp.where(causal, log_g[:, None, None] * (pos[:, None] - pos[None, :]), -jnp.inf))
    q_dec = jnp.exp(log_g[:, None] * (pos + 1.0))[..., None]
    k_dec = jnp.exp(log_g[:, None] * (CHUNK - 1.0 - pos))[..., None]
    c_dec = jnp.exp(log_g * CHUNK)[:, None, None]

    def step(state, inp):
        qc, kc, vc = inp
        sc = jnp.einsum('bhid,bhjd->bhij', qc, kc) * d_in
        inner = jnp.einsum('bhij,bhje->bhie', sc, vc)
        cross = jnp.einsum('bhid,bhde->bhie', qc, state) * q_dec
        state = state * c_dec + jnp.einsum('bhjd,bhje->bhde', kc * k_dec, vc)
        return state, inner + cross

    init = jnp.zeros((b, h, dk, v.shape[-1]), jnp.float32)
    _, out = lax.scan(step, init, (_to_chunks(q), _to_chunks(k), _to_chunks(v)))
    return _from_chunks(out)


def _gla(q, k, v, log_a):
    b, h, _, dk = q.shape
    pos = jnp.arange(CHUNK)
    causal = (pos[:, None] >= pos[None, :])[:, :, None]

    def step(state, inp):
        qc, kc, vc, ac = inp
        cum = jnp.cumsum(ac, axis=2)
        rel = jnp.exp(jnp.where(causal, cum[:, :, :, None, :] - cum[:, :, None, :, :], -jnp.inf))
        sc = jnp.einsum('bhid,bhjd,bhijd->bhij', qc, kc, rel)
        inner = jnp.einsum('bhij,bhje->bhie', sc, vc)
        cross = jnp.einsum('bhid,bhde->bhie', qc * jnp.exp(cum), state)
        last = cum[:, :, -1:, :]
        state = state * jnp.exp(last[:, :, 0, :, None]) + jnp.einsum('bhjd,bhje->bhde', kc * jnp.exp(last - cum), vc)
        return state, inner + cross

    init = jnp.zeros((b, h, dk, v.shape[-1]), jnp.float32)
    _, out = lax.scan(step, init, (_to_chunks(q), _to_chunks(k), _to_chunks(v), _to_chunks(log_a)))
    return _from_chunks(out)


def _mixer(h, w_in, lam_qk, lam_init, da_g, ret_g, gla_a2, gla_ab, gla_g, w_ba, w_bb, w_bc, w_o):
    f32 = jnp.float32
    dt = h.dtype
    b, s, _ = h.shape
    proj = h @ w_in
    sizes = [A_Q, A_K, A_V, B_Q, B_K, B_V, B_G, C_Q, C_K, C_V, C_G, C_A, GATE_COLS]
    idx = np.cumsum(sizes)[:-1].tolist()
    aq, ak, av, rq, rk, rv, rg, cq, ck, cv, cg, ca, gates = jnp.split(proj, idx, axis=-1)

    def two_maps(t):
        return t.reshape(b, s, N_HEADS, 2, DA_HEAD).transpose(0, 2, 3, 1, 4)
    lq = lam_qk.astype(f32)
    lam = jnp.exp(jnp.sum(lq[0] * lq[1])) - jnp.exp(jnp.sum(lq[2] * lq[3])) + lam_init
    ya = _diff_attention(two_maps(aq), two_maps(ak), _heads(av, N_HEADS), lam)
    ya = _merge_heads(_rmsnorm(ya, da_g) * (1.0 - lam_init)).astype(dt)

    rq_h = _rotary(_heads(rq, N_HEADS).astype(f32))
    rk_h = _rotary(_heads(rk, N_HEADS).astype(f32)) * (RET_DK ** -0.5)
    yb = _retention(rq_h, rk_h, _heads(rv, N_HEADS).astype(f32))
    yb = jax.nn.silu(rg) * _merge_heads(_rmsnorm(yb, ret_g)).astype(dt)

    log_a = jax.nn.log_sigmoid((ca @ gla_a2 + gla_ab).astype(f32)) / GLA_TAU
    yc = _gla(_heads(cq, N_HEADS).astype(f32) * (GLA_DK ** -0.5), _heads(ck, N_HEADS).astype(f32),
              _heads(cv, N_HEADS).astype(f32), _heads(log_a, N_HEADS))
    yc = jax.nn.silu(cg) * _merge_heads(_rmsnorm(yc, gla_g)).astype(dt)

    g = jax.nn.sigmoid(gates).reshape(b, s, N_BRANCH, D_MODEL)
    merged = g[:, :, 0] * (ya @ w_ba) + g[:, :, 1] * (yb @ w_bb) + g[:, :, 2] * (yc @ w_bc)
    return merged @ w_o


def setup_inputs(seed: int = 0) -> dict:
    key = jax.random.key(seed)
    ks = jax.random.split(key, 24)
    f32 = jnp.float32
    L = DEPTH

    def nrm(k, shape, scale):
        return jax.random.normal(k, shape, f32) * scale

    def gain(k, shape):
        return 1.0 + 0.02 * jax.random.normal(k, shape, f32)

    return {
        'x': nrm(ks[0], (BATCH, SEQ, D_MODEL), 1.0),
        'ffn1_norm': gain(ks[1], (L, D_MODEL)),
        'ffn1_w1': nrm(ks[2], (L, D_MODEL, D_FF), D_MODEL ** -0.5),
        'ffn1_w3': nrm(ks[3], (L, D_MODEL, D_FF), D_MODEL ** -0.5),
        'ffn1_w2': nrm(ks[4], (L, D_FF, D_MODEL), D_FF ** -0.5),
        'mix_norm': gain(ks[5], (L, D_MODEL)),
        'w_in': nrm(ks[6], (L, D_MODEL, IN_COLS), D_MODEL ** -0.5),
        'lam_qk': nrm(ks[7], (L, 4, DA_HEAD), 0.1),
        'da_norm': gain(ks[8], (L, DA_VDIM)),
        'ret_norm': gain(ks[9], (L, RET_DV)),
        'gla_a2': nrm(ks[10], (L, GLA_RANK, C_K), GLA_RANK ** -0.5),
        'gla_a_bias': nrm(ks[11], (L, C_K), 0.1),
        'gla_norm': gain(ks[12], (L, GLA_DV)),
        'w_branch_a': nrm(ks[13], (L, A_V, D_MODEL), A_V ** -0.5),
        'w_branch_b': nrm(ks[14], (L, B_V, D_MODEL), B_V ** -0.5),
        'w_branch_c': nrm(ks[15], (L, C_V, D_MODEL), C_V ** -0.5),
        'w_out': nrm(ks[16], (L, D_MODEL, D_MODEL), D_MODEL ** -0.5),
        'ffn2_norm': gain(ks[17], (L, D_MODEL)),
        'ffn2_w1': nrm(ks[18], (L, D_MODEL, D_FF), D_MODEL ** -0.5),
        'ffn2_w3': nrm(ks[19], (L, D_MODEL, D_FF), D_MODEL ** -0.5),
        'ffn2_w2': nrm(ks[20], (L, D_FF, D_MODEL), D_FF ** -0.5),
        'final_norm': gain(ks[21], (D_MODEL,)),
    }


def reference(x, ffn1_norm, ffn1_w1, ffn1_w3, ffn1_w2, mix_norm, w_in, lam_qk, da_norm, ret_norm,
              gla_a2, gla_a_bias, gla_norm, w_branch_a, w_branch_b, w_branch_c, w_out,
              ffn2_norm, ffn2_w1, ffn2_w3, ffn2_w2, final_norm):
    for l in range(DEPTH):
        lam_init = 0.8 - 0.6 * math.exp(-0.3 * l)
        x = x + 0.5 * _swiglu(_rmsnorm(x, ffn1_norm[l]), ffn1_w1[l], ffn1_w3[l], ffn1_w2[l])
        h = _rmsnorm(x, mix_norm[l])
        x = x + _mixer(h, w_in[l], lam_qk[l], lam_init, da_norm[l], ret_norm[l], gla_a2[l],
                       gla_a_bias[l], gla_norm[l], w_branch_a[l], w_branch_b[l], w_branch_c[l], w_out[l])
        x = x + 0.5 * _swiglu(_rmsnorm(x, ffn2_norm[l]), ffn2_w1[l], ffn2_w3[l], ffn2_w2[l])
    return _rmsnorm(x, final_norm)
```

```python
import functools
import math

import jax
import jax.numpy as jnp
from jax import lax
from jax.experimental import pallas as pl
from jax.experimental.pallas import tpu as pltpu

F32 = jnp.float32
BF16 = jnp.bfloat16

D_MODEL = 1024
N_HEADS = 4
DEPTH = 4
CHUNK = 64
DA_HEAD = 64
HEAD_DV = 128
LIN_DK = 64
GLA_RANK = 16
GLA_TAU = 16.0
ROPE_BASE = 10000.0
D_FF = 2752
EPS = 1e-6

LANES = 128
MXU_DIM = 256

D_FF_PAD = ((D_FF + MXU_DIM - 1) // MXU_DIM) * MXU_DIM
QK_A = N_HEADS * 2 * DA_HEAD
V_W = N_HEADS * HEAD_DV
QK_L = N_HEADS * LIN_DK
GATE_W = 3 * D_MODEL
OFF_A = GATE_W
OFF_B = OFF_A + 2 * QK_A + V_W
OFF_C = OFF_B + 2 * QK_L + 2 * V_W
PROJ_W = OFF_C + 2 * QK_L + 2 * V_W

TM_FFN = 1024
TF_FFN = MXU_DIM
TM_PROJ = 1024
TN_PROJ = 512
TQ_ATT = 256
T_LIN = 512
SUB = 16
TM_MERGE = 512
VMEM_LIMIT = 48 * 1024 * 1024


def _rms(xf, g):
    return xf * lax.rsqrt(jnp.mean(xf * xf, axis=-1, keepdims=True) + EPS) * g


def _dot(a, b):
    return jnp.dot(a, b, preferred_element_type=F32)


def _dot_nt(a, b):
    return lax.dot_general(a, b, (((1,), (1,)), ((), ())), preferred_element_type=F32)


def _dot_tn(a, b):
    return lax.dot_general(a, b, (((0,), (0,)), ((), ())), preferred_element_type=F32)


def _ffn_kernel(x_ref, g_ref, w1_ref, w3_ref, w2_ref, gf_ref, o_ref, h_scr, acc_scr, *, final):
    k = pl.program_id(1)

    @pl.when(k == 0)
    def _():
        h_scr[...] = _rms(x_ref[...], g_ref[...]).astype(BF16)
        acc_scr[...] = jnp.zeros_like(acc_scr)

    h = h_scr[...]
    a = _dot(h, w1_ref[...])
    b = _dot(h, w3_ref[...])
    t = (a * jax.nn.sigmoid(a) * b).astype(BF16)
    acc_scr[...] += _dot(t, w2_ref[...])

    @pl.when(k == pl.num_programs(1) - 1)
    def _():
        y = x_ref[...] + 0.5 * acc_scr[...]
        if final:
            y = _rms(y, gf_ref[...])
        o_ref[...] = y


def _ffn(x2, g, w1, w3, w2, gf, final):
    n = x2.shape[0]
    grid = (n // TM_FFN, D_FF_PAD // TF_FFN)
    return pl.pallas_call(
        functools.partial(_ffn_kernel, final=final),
        grid=grid,
        in_specs=[
            pl.BlockSpec((TM_FFN, D_MODEL), lambda i, k: (i, 0)),
            pl.BlockSpec((1, D_MODEL), lambda i, k: (0, 0)),
            pl.BlockSpec((D_MODEL, TF_FFN), lambda i, k: (0, k)),
            pl.BlockSpec((D_MODEL, TF_FFN), lambda i, k: (0, k)),
            pl.BlockSpec((TF_FFN, D_MODEL), lambda i, k: (k, 0)),
            pl.BlockSpec((1, D_MODEL), lambda i, k: (0, 0)),
        ],
        out_specs=pl.BlockSpec((TM_FFN, D_MODEL), lambda i, k: (i, 0)),
        out_shape=jax.ShapeDtypeStruct((n, D_MODEL), F32),
        scratch_shapes=[pltpu.VMEM((TM_FFN, D_MODEL), BF16), pltpu.VMEM((TM_FFN, D_MODEL), F32)],
        compiler_params=pltpu.CompilerParams(
            dimension_semantics=("parallel", "arbitrary"), vmem_limit_bytes=VMEM_LIMIT),
        name="ffn",
    )(x2, g, w1, w3, w2, gf)


def _inproj_kernel(x_ref, g_ref, w_ref, wca_ref, proj_ref, ca_ref, h_scr):
    j = pl.program_id(1)

    @pl.when(j == 0)
    def _():
        h = _rms(x_ref[...], g_ref[...]).astype(BF16)
        h_scr[...] = h
        ca_ref[...] = _dot(h, wca_ref[...])

    proj_ref[...] = _dot(h_scr[...], w_ref[...]).astype(BF16)


def _inproj(x2, g, w, wca):
    n = x2.shape[0]
    grid = (n // TM_PROJ, PROJ_W // TN_PROJ)
    return pl.pallas_call(
        _inproj_kernel,
        grid=grid,
        in_specs=[
            pl.BlockSpec((TM_PROJ, D_MODEL), lambda i, j: (i, 0)),
            pl.BlockSpec((1, D_MODEL), lambda i, j: (0, 0)),
            pl.BlockSpec((D_MODEL, TN_PROJ), lambda i, j: (0, j)),
            pl.BlockSpec((D_MODEL, LANES), lambda i, j: (0, 0)),
        ],
        out_specs=[
            pl.BlockSpec((TM_PROJ, TN_PROJ), lambda i, j: (i, j)),
            pl.BlockSpec((TM_PROJ, LANES), lambda i, j: (i, 0)),
        ],
        out_shape=[
            jax.ShapeDtypeStruct((n, PROJ_W), BF16),
            jax.ShapeDtypeStruct((n, LANES), F32),
        ],
        scratch_shapes=[pltpu.VMEM((TM_PROJ, D_MODEL), BF16)],
        compiler_params=pltpu.CompilerParams(
            dimension_semantics=("parallel", "arbitrary"), vmem_limit_bytes=VMEM_LIMIT),
        name="inproj",
    )(x2, g, w, wca)


def _attn_kernel(lamqk_ref, g_ref, q_ref, k_ref, v_ref, o_ref, m_scr, l_scr, acc_scr, *, lam_init):
    tq = TQ_ATT
    qi = pl.program_id(2)
    q = q_ref[0]
    qs = q * jnp.asarray(DA_HEAD ** -0.5, BF16)
    lane = lax.broadcasted_iota(jnp.int32, (tq, 2 * DA_HEAD), 1)
    zero = jnp.zeros_like(qs)
    qq = jnp.concatenate(
        [jnp.where(lane < DA_HEAD, qs, zero), jnp.where(lane >= DA_HEAD, qs, zero)], axis=0)

    m_scr[...] = jnp.full(m_scr.shape, -jnp.inf, F32)
    l_scr[...] = jnp.zeros_like(l_scr)
    acc_scr[...] = jnp.zeros_like(acc_scr)

    def step(j, masked):
        start = pl.multiple_of(j * tq, tq)
        kt = k_ref[0, pl.ds(start, tq), :]
        vt = v_ref[0, pl.ds(start, tq), :]
        s = _dot_nt(qq, kt)
        if masked:
            row = lax.broadcasted_iota(jnp.int32, (2 * tq, tq), 0)
            col = lax.broadcasted_iota(jnp.int32, (2 * tq, tq), 1)
            vis = (col // CHUNK) <= ((row % tq) // CHUNK)
            s = jnp.where(vis, s, -jnp.inf)
        m_prev = m_scr[...]
        m_new = jnp.maximum(m_prev, jnp.max(s, axis=-1, keepdims=True))
        alpha = jnp.exp(m_prev - m_new)
        p = jnp.exp(s - m_new)
        l_scr[...] = alpha * l_scr[...] + jnp.sum(p, axis=-1, keepdims=True)
        acc_scr[...] = alpha * acc_scr[...] + _dot(p.astype(BF16), vt)
        m_scr[...] = m_new

    def body(j, carry):
        step(j, False)
        return carry

    lax.fori_loop(0, qi, body, 0)
    step(qi, True)

    lq = lamqk_ref[...]
    lam = (jnp.exp(jnp.sum(lq[0:1] * lq[1:2], axis=-1, keepdims=True))
           - jnp.exp(jnp.sum(lq[2:3] * lq[3:4], axis=-1, keepdims=True)) + lam_init)
    a = acc_scr[...] / l_scr[...]
    out = a[:tq] - lam * a[tq:]
    o_ref[0] = (_rms(out, g_ref[...]) * (1.0 - lam_init)).astype(BF16)


def _attn(proj3, lamqk, g, lam_init):
    b, s, _ = proj3.shape
    qb, kb, vb = OFF_A // LANES, (OFF_A + QK_A) // LANES, (OFF_A + 2 * QK_A) // LANES
    return pl.pallas_call(
        functools.partial(_attn_kernel, lam_init=lam_init),
        grid=(b, N_HEADS, s // TQ_ATT),
        in_specs=[
            pl.BlockSpec((4, DA_HEAD), lambda bi, h, qi: (0, 0)),
            pl.BlockSpec((1, HEAD_DV), lambda bi, h, qi: (0, 0)),
            pl.BlockSpec((1, TQ_ATT, LANES), lambda bi, h, qi: (bi, qi, qb + h)),
            pl.BlockSpec((1, s, LANES), lambda bi, h, qi: (bi, 0, kb + h)),
            pl.BlockSpec((1, s, LANES), lambda bi, h, qi: (bi, 0, vb + h)),
        ],
        out_specs=pl.BlockSpec((1, TQ_ATT, HEAD_DV), lambda bi, h, qi: (bi, qi, h)),
        out_shape=jax.ShapeDtypeStruct((b, s, V_W), BF16),
        scratch_shapes=[
            pltpu.VMEM((2 * TQ_ATT, 1), F32),
            pltpu.VMEM((2 * TQ_ATT, 1), F32),
            pltpu.VMEM((2 * TQ_ATT, HEAD_DV), F32),
        ],
        compiler_params=pltpu.CompilerParams(
            dimension_semantics=("parallel", "parallel", "arbitrary"), vmem_limit_bytes=VMEM_LIMIT),
        name="diff_attn",
    )(lamqk, g, proj3, proj3, proj3)


def _stack_heads(t):
    lane = lax.broadcasted_iota(jnp.int32, t.shape, 1)
    return jnp.concatenate(
        [jnp.where((lane // LIN_DK) == h, t, 0.0) for h in range(N_HEADS)], axis=0).astype(BF16)


def _lin_chunk(q, k, v, la, st_ref):
    c = CHUNK
    r = lax.broadcasted_iota(jnp.int32, (c, c), 0)
    cc = lax.broadcasted_iota(jnp.int32, (c, c), 1)
    tri = jnp.where(cc <= r, 1.0, 0.0).astype(BF16)
    la_hi = la.astype(BF16)
    la_lo = (la - la_hi.astype(F32)).astype(BF16)
    cum = _dot(tri, la_hi) + _dot(tri, la_lo)

    rowk = lax.broadcasted_iota(jnp.int32, (c, QK_L), 0)
    srow = lax.broadcasted_iota(jnp.int32, (N_HEADS * SUB, c), 0)
    scol = lax.broadcasted_iota(jnp.int32, (N_HEADS * SUB, c), 1)
    inner = []
    for i in range(c // SUB):
        lo, hi = SUB * i, SUB * (i + 1)
        ref = cum[lo - 1:lo] if i > 0 else jnp.zeros((1, QK_L), F32)
        q_i = q[lo:hi] * jnp.exp(cum[lo:hi] - ref)
        k_i = jnp.where(rowk < hi, k * jnp.exp(ref - cum), 0.0)
        sc = _dot_nt(_stack_heads(q_i), k_i.astype(BF16))
        sc = jnp.where((scol - lo) <= (srow % SUB), sc, 0.0)
        o = _dot(sc.astype(BF16), v)
        inner.append(jnp.concatenate(
            [o[SUB * h:SUB * (h + 1), HEAD_DV * h:HEAD_DV * (h + 1)] for h in range(N_HEADS)], axis=1))
    inner = jnp.concatenate(inner, axis=0)

    st = st_ref[...]
    cr = _dot_nt(_stack_heads(q * jnp.exp(cum)), st.astype(BF16))
    cross = jnp.concatenate([cr[c * h:c * (h + 1)] for h in range(N_HEADS)], axis=1)
    last = cum[c - 1:c]
    ke = _stack_heads(k * jnp.exp(last - cum))
    vs = jnp.concatenate([v[:, HEAD_DV * h:HEAD_DV * (h + 1)] for h in range(N_HEADS)], axis=0)
    st_ref[...] = st * jnp.exp(last) + _dot_tn(vs, ke)
    return inner + cross


def _head_rms(o, g):
    return jnp.concatenate(
        [_rms(o[:, HEAD_DV * h:HEAD_DV * (h + 1)], g) for h in range(N_HEADS)], axis=1)


def _swap_halves(t):
    lane = lax.broadcasted_iota(jnp.int32, t.shape, 1)
    half = LIN_DK // 2
    return jnp.where((lane % LIN_DK) < half,
                     pltpu.roll(t, QK_L - half, 1), pltpu.roll(t, half, 1))


def _ret_kernel(q_ref, k_ref, v_ref, gate_ref, cos_ref, sin_ref, lg_ref, gn_ref, o_ref, st_scr):
    @pl.when(pl.program_id(1) == 0)
    def _():
        st_scr[...] = jnp.zeros_like(st_scr)

    def chunk(ci, carry):
        sl = pl.ds(pl.multiple_of(ci * CHUNK, CHUNK), CHUNK)
        cos = cos_ref[sl, :]
        sin = sin_ref[sl, :]
        q = q_ref[0, sl, :].astype(F32)
        k = k_ref[0, sl, :].astype(F32)
        q = q * cos + _swap_halves(q) * sin
        k = (k * cos + _swap_halves(k) * sin) * (LIN_DK ** -0.5)
        la = jnp.broadcast_to(lg_ref[...], (CHUNK, QK_L))
        o = _lin_chunk(q, k, v_ref[0, sl, :], la, st_scr)
        y = jax.nn.silu(gate_ref[0, sl, :].astype(F32)) * _head_rms(o, gn_ref[...])
        o_ref[0, sl, :] = y.astype(BF16)
        return carry

    lax.fori_loop(0, T_LIN // CHUNK, chunk, 0)


def _gla_kernel(q_ref, k_ref, v_ref, gate_ref, ca_ref, a2_ref, ab_ref, gn_ref, o_ref, st_scr):
    @pl.when(pl.program_id(1) == 0)
    def _():
        st_scr[...] = jnp.zeros_like(st_scr)

    def chunk(ci, carry):
        sl = pl.ds(pl.multiple_of(ci * CHUNK, CHUNK), CHUNK)
        z = _dot(ca_ref[0, sl, :].astype(BF16), a2_ref[...]) + ab_ref[...]
        la = (jnp.minimum(z, 0.0) - jnp.log(1.0 + jnp.exp(-jnp.abs(z)))) * (1.0 / GLA_TAU)
        q = q_ref[0, sl, :].astype(F32) * (LIN_DK ** -0.5)
        k = k_ref[0, sl, :].astype(F32)
        o = _lin_chunk(q, k, v_ref[0, sl, :], la, st_scr)
        y = jax.nn.silu(gate_ref[0, sl, :].astype(F32)) * _head_rms(o, gn_ref[...])
        o_ref[0, sl, :] = y.astype(BF16)
        return carry

    lax.fori_loop(0, T_LIN // CHUNK, chunk, 0)


def _lin_specs(off):
    qb, kb = off // QK_L, (off + QK_L) // QK_L
    vb, gb = (off + 2 * QK_L) // V_W, (off + 2 * QK_L + V_W) // V_W
    return [
        pl.BlockSpec((1, T_LIN, QK_L), lambda bi, t: (bi, t, qb)),
        pl.BlockSpec((1, T_LIN, QK_L), lambda bi, t: (bi, t, kb)),
        pl.BlockSpec((1, T_LIN, V_W), lambda bi, t: (bi, t, vb)),
        pl.BlockSpec((1, T_LIN, V_W), lambda bi, t: (bi, t, gb)),
    ]


def _lin_call(kern, name, proj3, extra_specs, extra_args, off):
    b, s, _ = proj3.shape
    return pl.pallas_call(
        kern,
        grid=(b, s // T_LIN),
        in_specs=_lin_specs(off) + extra_specs,
        out_specs=pl.BlockSpec((1, T_LIN, V_W), lambda bi, t: (bi, t, 0)),
        out_shape=jax.ShapeDtypeStruct((b, s, V_W), BF16),
        scratch_shapes=[pltpu.VMEM((HEAD_DV, QK_L), F32)],
        compiler_params=pltpu.CompilerParams(
            dimension_semantics=("parallel", "arbitrary"), vmem_limit_bytes=VMEM_LIMIT),
        name=name,
    )(proj3, proj3, proj3, proj3, *extra_args)


def _retention(proj3, cos, sin, lg, gn):
    extra = [
        pl.BlockSpec((T_LIN, QK_L), lambda bi, t: (t, 0)),
        pl.BlockSpec((T_LIN, QK_L), lambda bi, t: (t, 0)),
        pl.BlockSpec((1, QK_L), lambda bi, t: (0, 0)),
        pl.BlockSpec((1, HEAD_DV), lambda bi, t: (0, 0)),
    ]
    return _lin_call(_ret_kernel, "retention", proj3, extra, (cos, sin, lg, gn), OFF_B)


def _gla(proj3, ca3, a2, ab, gn):
    extra = [
        pl.BlockSpec((1, T_LIN, LANES), lambda bi, t: (bi, t, 0)),
        pl.BlockSpec((LANES, QK_L), lambda bi, t: (0, 0)),
        pl.BlockSpec((1, QK_L), lambda bi, t: (0, 0)),
        pl.BlockSpec((1, HEAD_DV), lambda bi, t: (0, 0)),
    ]
    return _lin_call(_gla_kernel, "gla", proj3, extra, (ca3, a2, ab, gn), OFF_C)


def _merge_kernel(x_ref, g0_ref, g1_ref, g2_ref, ya_ref, yb_ref, yc_ref,
                  wa_ref, wb_ref, wc_ref, wo_ref, o_ref):
    def sig(ref):
        return jax.nn.sigmoid(ref[...].astype(F32))

    merged = (sig(g0_ref) * _dot(ya_ref[...], wa_ref[...])
              + sig(g1_ref) * _dot(yb_ref[...], wb_ref[...])
              + sig(g2_ref) * _dot(yc_ref[...], wc_ref[...]))
    o_ref[...] = x_ref[...] + _dot(merged.astype(BF16), wo_ref[...])


def _merge(x2, proj2, ya, yb, yc, wa, wb, wc, wo):
    n = x2.shape[0]
    row = lambda w: pl.BlockSpec((TM_MERGE, w), lambda i: (i, 0))
    full = lambda a: pl.BlockSpec(a.shape, lambda i: (0, 0))
    return pl.pallas_call(
        _merge_kernel,
        grid=(n // TM_MERGE,),
        in_specs=[
            row(D_MODEL),
            pl.BlockSpec((TM_MERGE, D_MODEL), lambda i: (i, 0)),
            pl.BlockSpec((TM_MERGE, D_MODEL), lambda i: (i, 1)),
            pl.BlockSpec((TM_MERGE, D_MODEL), lambda i: (i, 2)),
            row(V_W), row(V_W), row(V_W),
            full(wa), full(wb), full(wc), full(wo),
        ],
        out_specs=row(D_MODEL),
        out_shape=jax.ShapeDtypeStruct((n, D_MODEL), F32),
        compiler_params=pltpu.CompilerParams(
            dimension_semantics=("parallel",), vmem_limit_bytes=VMEM_LIMIT),
        name="merge",
    )(x2, proj2, proj2, proj2, ya, yb, yc, wa, wb, wc, wo)


def _rope_tables(s):
    half = LIN_DK // 2
    inv = ROPE_BASE ** (-jnp.arange(0, LIN_DK, 2, dtype=F32) / LIN_DK)
    ang = jnp.arange(s, dtype=F32)[:, None] * inv[None, :]
    cos, sin = jnp.cos(ang), jnp.sin(ang)
    del half
    cos_t = jnp.tile(jnp.concatenate([cos, cos], axis=-1), (1, N_HEADS))
    sin_t = jnp.tile(jnp.concatenate([-sin, sin], axis=-1), (1, N_HEADS))
    return cos_t, sin_t


def kernel(x, ffn1_norm, ffn1_w1, ffn1_w3, ffn1_w2, mix_norm, w_in, lam_qk, da_norm, ret_norm,
           gla_a2, gla_a_bias, gla_norm, w_branch_a, w_branch_b, w_branch_c, w_out,
           ffn2_norm, ffn2_w1, ffn2_w3, ffn2_w2, final_norm):
    b, s, d = x.shape
    assert d == D_MODEL and s % T_LIN == 0 and s % TQ_ATT == 0 and (b * s) % TM_FFN == 0
    pad_f = D_FF_PAD - D_FF

    def prep_ffn(w1, w3, w2):
        return (jnp.pad(w1, ((0, 0), (0, 0), (0, pad_f))).astype(BF16),
                jnp.pad(w3, ((0, 0), (0, 0), (0, pad_f))).astype(BF16),
                jnp.pad(w2, ((0, 0), (0, pad_f), (0, 0))).astype(BF16))

    f1 = prep_ffn(ffn1_w1, ffn1_w3, ffn1_w2)
    f2 = prep_ffn(ffn2_w1, ffn2_w3, ffn2_w2)

    n_abc = OFF_C + 2 * QK_L + 2 * V_W - GATE_W
    w_main = jnp.concatenate(
        [w_in[:, :, n_abc + GLA_RANK:], w_in[:, :, :n_abc]], axis=-1).astype(BF16)
    w_ca = jnp.pad(w_in[:, :, n_abc:n_abc + GLA_RANK],
                   ((0, 0), (0, 0), (0, LANES - GLA_RANK))).astype(BF16)
    a2 = jnp.pad(gla_a2, ((0, 0), (0, LANES - GLA_RANK), (0, 0))).astype(BF16)
    wa, wb, wc, wo = (w.astype(BF16) for w in (w_branch_a, w_branch_b, w_branch_c, w_out))

    cos_t, sin_t = _rope_tables(s)
    log_g = jnp.log1p(-jnp.exp2(-5.0 - jnp.arange(N_HEADS, dtype=F32)))
    lg = jnp.repeat(log_g, LIN_DK)[None, :]
    fin = final_norm[None, :]

    x2 = x.reshape(b * s, d)
    for l in range(DEPTH):
        lam_init = 0.8 - 0.6 * math.exp(-0.3 * l)
        x2 = _ffn(x2, ffn1_norm[l][None, :], f1[0][l], f1[1][l], f1[2][l], fin, False)
        proj2, ca2 = _inproj(x2, mix_norm[l][None, :], w_main[l], w_ca[l])
        proj3 = proj2.reshape(b, s, PROJ_W)
        ya = _attn(proj3, lam_qk[l], da_norm[l][None, :], lam_init)
        yb = _retention(proj3, cos_t, sin_t, lg, ret_norm[l][None, :])
        yc = _gla(proj3, ca2.reshape(b, s, LANES), a2[l], gla_a_bias[l][None, :], gla_norm[l][None, :])
        x2 = _merge(x2, proj2, ya.reshape(b * s, V_W), yb.reshape(b * s, V_W), yc.reshape(b * s, V_W),
                    wa[l], wb[l], wc[l], wo[l])
        x2 = _ffn(x2, ffn2_norm[l][None, :], f2[0][l], f2[1][l], f2[2][l], fin, l == DEPTH - 1)
    return x2.reshape(b, s, d)
```

```python
import functools
import math

import jax
import jax.numpy as jnp
from jax import lax
from jax.experimental import pallas as pl
from jax.experimental.pallas import tpu as pltpu

F32 = jnp.float32
BF16 = jnp.bfloat16

D_MODEL = 1024
N_HEADS = 4
DEPTH = 4
CHUNK = 64
DA_HEAD = 64
HEAD_DV = 128
LIN_DK = 64
GLA_RANK = 16
GLA_TAU = 16.0
ROPE_BASE = 10000.0
D_FF = 2752
EPS = 1e-6

LANES = 128
MXU_DIM = 256

D_FF_PAD = ((D_FF + MXU_DIM - 1) // MXU_DIM) * MXU_DIM
QK_A = N_HEADS * 2 * DA_HEAD
V_W = N_HEADS * HEAD_DV
QK_L = N_HEADS * LIN_DK
GATE_W = 3 * D_MODEL
OFF_A = GATE_W
OFF_B = OFF_A + 2 * QK_A + V_W
OFF_C = OFF_B + 2 * QK_L + 2 * V_W
PROJ_W = OFF_C + 2 * QK_L + 2 * V_W

TM_FFN = 1024
TF_FFN = MXU_DIM
TM_PROJ = 1024
TN_PROJ = 512
TQ_ATT = 256
TK_ATT = 512
T_LIN = 512
SUB = 16
TM_MERGE = 512
VMEM_LIMIT = 48 * 1024 * 1024


def _rms(xf, g):
    return xf * lax.rsqrt(jnp.mean(xf * xf, axis=-1, keepdims=True) + EPS) * g


def _dot(a, b):
    return jnp.dot(a, b, preferred_element_type=F32)


def _dot_nt(a, b):
    return lax.dot_general(a, b, (((1,), (1,)), ((), ())), preferred_element_type=F32)


def _dot_tn(a, b):
    return lax.dot_general(a, b, (((0,), (0,)), ((), ())), preferred_element_type=F32)


def _ffn_kernel(x_ref, g_ref, w1_ref, w3_ref, w2_ref, gf_ref, o_ref, h_scr, acc_scr, *, final):
    k = pl.program_id(1)

    @pl.when(k == 0)
    def _():
        h_scr[...] = _rms(x_ref[...], g_ref[...]).astype(BF16)
        acc_scr[...] = jnp.zeros_like(acc_scr)

    h = h_scr[...]
    a = _dot(h, w1_ref[...])
    b = _dot(h, w3_ref[...])
    t = (a * jax.nn.sigmoid(a) * b).astype(BF16)
    acc_scr[...] += _dot(t, w2_ref[...])

    @pl.when(k == pl.num_programs(1) - 1)
    def _():
        y = x_ref[...] + 0.5 * acc_scr[...]
        if final:
            y = _rms(y, gf_ref[...])
        o_ref[...] = y


def _ffn(x2, g, w1, w3, w2, gf, final):
    n = x2.shape[0]
    grid = (n // TM_FFN, D_FF_PAD // TF_FFN)
    return pl.pallas_call(
        functools.partial(_ffn_kernel, final=final),
        grid=grid,
        in_specs=[
            pl.BlockSpec((TM_FFN, D_MODEL), lambda i, k: (i, 0)),
            pl.BlockSpec((1, D_MODEL), lambda i, k: (0, 0)),
            pl.BlockSpec((D_MODEL, TF_FFN), lambda i, k: (0, k)),
            pl.BlockSpec((D_MODEL, TF_FFN), lambda i, k: (0, k)),
            pl.BlockSpec((TF_FFN, D_MODEL), lambda i, k: (k, 0)),
            pl.BlockSpec((1, D_MODEL), lambda i, k: (0, 0)),
        ],
        out_specs=pl.BlockSpec((TM_FFN, D_MODEL), lambda i, k: (i, 0)),
        out_shape=jax.ShapeDtypeStruct((n, D_MODEL), F32),
        scratch_shapes=[pltpu.VMEM((TM_FFN, D_MODEL), BF16), pltpu.VMEM((TM_FFN, D_MODEL), F32)],
        compiler_params=pltpu.CompilerParams(
            dimension_semantics=("parallel", "arbitrary"), vmem_limit_bytes=VMEM_LIMIT),
        name="ffn",
    )(x2, g, w1, w3, w2, gf)


def _inproj_kernel(x_ref, g_ref, w_ref, wca_ref, proj_ref, ca_ref, h_scr):
    j = pl.program_id(1)

    @pl.when(j == 0)
    def _():
        h = _rms(x_ref[...], g_ref[...]).astype(BF16)
        h_scr[...] = h
        ca_ref[...] = _dot(h, wca_ref[...])

    proj_ref[...] = _dot(h_scr[...], w_ref[...]).astype(BF16)


def _inproj(x2, g, w, wca):
    n = x2.shape[0]
    grid = (n // TM_PROJ, PROJ_W // TN_PROJ)
    return pl.pallas_call(
        _inproj_kernel,
        grid=grid,
        in_specs=[
            pl.BlockSpec((TM_PROJ, D_MODEL), lambda i, j: (i, 0)),
            pl.BlockSpec((1, D_MODEL), lambda i, j: (0, 0)),
            pl.BlockSpec((D_MODEL, TN_PROJ), lambda i, j: (0, j)),
            pl.BlockSpec((D_MODEL, LANES), lambda i, j: (0, 0)),
        ],
        out_specs=[
            pl.BlockSpec((TM_PROJ, TN_PROJ), lambda i, j: (i, j)),
            pl.BlockSpec((TM_PROJ, LANES), lambda i, j: (i, 0)),
        ],
        out_shape=[
            jax.ShapeDtypeStruct((n, PROJ_W), BF16),
            jax.ShapeDtypeStruct((n, LANES), F32),
        ],
        scratch_shapes=[pltpu.VMEM((TM_PROJ, D_MODEL), BF16)],
        compiler_params=pltpu.CompilerParams(
            dimension_semantics=("parallel", "arbitrary"), vmem_limit_bytes=VMEM_LIMIT),
        name="inproj",
    )(x2, g, w, wca)


def _attn_kernel(lamqk_ref, g_ref, q_ref, k_ref, v_ref, o_ref,
                 qq_scr, s0_scr, s1_scr, m_scr, l_scr, acc_scr, *, lam_init):
    tq, tk = TQ_ATT, TK_ATT
    qi = pl.program_id(2)
    q = q_ref[0].astype(F32) * (DA_HEAD ** -0.5 * math.log2(math.e))
    lane = lax.broadcasted_iota(jnp.int32, (tq, 2 * DA_HEAD), 1)
    qq_scr[...] = jnp.concatenate(
        [jnp.where(lane < DA_HEAD, q, 0.0), jnp.where(lane >= DA_HEAD, q, 0.0)], axis=0).astype(BF16)

    m_scr[...] = jnp.full(m_scr.shape, -jnp.inf, F32)
    l_scr[...] = jnp.zeros_like(l_scr)
    acc_scr[...] = jnp.zeros_like(acc_scr)

    def scores(j, s_ref):
        start = pl.multiple_of(j * tk, tk)
        s_ref[...] = _dot_nt(qq_scr[...], k_ref[0, pl.ds(start, tk), :])

    def consume(j, s_ref, masked):
        start = pl.multiple_of(j * tk, tk)
        s = s_ref[...]
        if masked:
            row = lax.broadcasted_iota(jnp.int32, (2 * tq, tk), 0)
            col = lax.broadcasted_iota(jnp.int32, (2 * tq, tk), 1)
            vis = ((start + col) // CHUNK) <= ((qi * tq + row % tq) // CHUNK)
            s = jnp.where(vis, s, -jnp.inf)
        m_prev = m_scr[...]
        m_new = jnp.maximum(m_prev, jnp.max(s, axis=-1, keepdims=True))
        alpha = jnp.exp2(m_prev - m_new)
        p = jnp.exp2(s - pltpu.repeat(m_new, tk // LANES, axis=1))
        psum = p[:, 0:LANES]
        for c in range(1, tk // LANES):
            psum = psum + p[:, c * LANES:(c + 1) * LANES]
        l_scr[...] = alpha * l_scr[...] + psum
        acc_scr[...] = alpha * acc_scr[...] + _dot(p.astype(BF16), v_ref[0, pl.ds(start, tk), :])
        m_scr[...] = m_new

    u = (qi * tq) // tk
    scores(0, s0_scr)

    def pair(i, carry):
        scores(2 * i + 1, s1_scr)
        consume(2 * i, s0_scr, False)
        scores(2 * i + 2, s0_scr)
        consume(2 * i + 1, s1_scr, False)
        return carry

    lax.fori_loop(0, u // 2, pair, 0)

    @pl.when(u % 2 == 0)
    def _():
        consume(u, s0_scr, True)

    @pl.when(u % 2 == 1)
    def _():
        scores(u, s1_scr)
        consume(u - 1, s0_scr, False)
        consume(u, s1_scr, True)

    lq = lamqk_ref[...]
    lam = (jnp.exp(jnp.sum(lq[0:1] * lq[1:2], axis=-1, keepdims=True))
           - jnp.exp(jnp.sum(lq[2:3] * lq[3:4], axis=-1, keepdims=True)) + lam_init)
    a = acc_scr[...] / jnp.sum(l_scr[...], axis=-1, keepdims=True)
    out = a[:tq] - lam * a[tq:]
    o_ref[0] = (_rms(out, g_ref[...]) * (1.0 - lam_init)).astype(BF16)


def _attn(proj3, lamqk, g, lam_init):
    b, s, _ = proj3.shape
    qb, kb, vb = OFF_A // LANES, (OFF_A + QK_A) // LANES, (OFF_A + 2 * QK_A) // LANES
    return pl.pallas_call(
        functools.partial(_attn_kernel, lam_init=lam_init),
        grid=(b, N_HEADS, s // TQ_ATT),
        in_specs=[
            pl.BlockSpec((4, DA_HEAD), lambda bi, h, qi: (0, 0)),
            pl.BlockSpec((1, HEAD_DV), lambda bi, h, qi: (0, 0)),
            pl.BlockSpec((1, TQ_ATT, LANES), lambda bi, h, qi: (bi, qi, qb + h)),
            pl.BlockSpec((1, s, LANES), lambda bi, h, qi: (bi, 0, kb + h)),
            pl.BlockSpec((1, s, LANES), lambda bi, h, qi: (bi, 0, vb + h)),
        ],
        out_specs=pl.BlockSpec((1, TQ_ATT, HEAD_DV), lambda bi, h, qi: (bi, qi, h)),
        out_shape=jax.ShapeDtypeStruct((b, s, V_W), BF16),
        scratch_shapes=[
            pltpu.VMEM((2 * TQ_ATT, 2 * DA_HEAD), BF16),
            pltpu.VMEM((2 * TQ_ATT, TK_ATT), F32),
            pltpu.VMEM((2 * TQ_ATT, TK_ATT), F32),
            pltpu.VMEM((2 * TQ_ATT, LANES), F32),
            pltpu.VMEM((2 * TQ_ATT, LANES), F32),
            pltpu.VMEM((2 * TQ_ATT, HEAD_DV), F32),
        ],
        compiler_params=pltpu.CompilerParams(
            dimension_semantics=("parallel", "parallel", "arbitrary"), vmem_limit_bytes=VMEM_LIMIT),
        name="diff_attn",
    )(lamqk, g, proj3, proj3, proj3)


def _stack_heads(t):
    lane = lax.broadcasted_iota(jnp.int32, t.shape, 1)
    return jnp.concatenate(
        [jnp.where((lane // LIN_DK) == h, t, 0.0) for h in range(N_HEADS)], axis=0).astype(BF16)


def _lin_chunk(q, k, v, la, st_ref):
    c = CHUNK
    r = lax.broadcasted_iota(jnp.int32, (c, c), 0)
    cc = lax.broadcasted_iota(jnp.int32, (c, c), 1)
    tri = jnp.where(cc <= r, 1.0, 0.0).astype(BF16)
    la_hi = la.astype(BF16)
    la_lo = (la - la_hi.astype(F32)).astype(BF16)
    cum = _dot(tri, la_hi) + _dot(tri, la_lo)

    rowk = lax.broadcasted_iota(jnp.int32, (c, QK_L), 0)
    srow = lax.broadcasted_iota(jnp.int32, (N_HEADS * SUB, c), 0)
    scol = lax.broadcasted_iota(jnp.int32, (N_HEADS * SUB, c), 1)
    inner = []
    for i in range(c // SUB):
        lo, hi = SUB * i, SUB * (i + 1)
        ref = cum[lo - 1:lo] if i > 0 else jnp.zeros((1, QK_L), F32)
        q_i = q[lo:hi] * jnp.exp(cum[lo:hi] - ref)
        k_i = jnp.where(rowk < hi, k * jnp.exp(ref - cum), 0.0)
        sc = _dot_nt(_stack_heads(q_i), k_i.astype(BF16))
        sc = jnp.where((scol - lo) <= (srow % SUB), sc, 0.0)
        o = _dot(sc.astype(BF16), v)
        inner.append(jnp.concatenate(
            [o[SUB * h:SUB * (h + 1), HEAD_DV * h:HEAD_DV * (h + 1)] for h in range(N_HEADS)], axis=1))
    inner = jnp.concatenate(inner, axis=0)

    st = st_ref[...]
    cr = _dot_nt(_stack_heads(q * jnp.exp(cum)), st.astype(BF16))
    cross = jnp.concatenate([cr[c * h:c * (h + 1)] for h in range(N_HEADS)], axis=1)
    last = cum[c - 1:c]
    ke = _stack_heads(k * jnp.exp(last - cum))
    vs = jnp.concatenate([v[:, HEAD_DV * h:HEAD_DV * (h + 1)] for h in range(N_HEADS)], axis=0)
    st_ref[...] = st * jnp.exp(last) + _dot_tn(vs, ke)
    return inner + cross


def _head_rms(o, g):
    return jnp.concatenate(
        [_rms(o[:, HEAD_DV * h:HEAD_DV * (h + 1)], g) for h in range(N_HEADS)], axis=1)


def _swap_halves(t):
    lane = lax.broadcasted_iota(jnp.int32, t.shape, 1)
    half = LIN_DK // 2
    return jnp.where((lane % LIN_DK) < half,
                     pltpu.roll(t, QK_L - half, 1), pltpu.roll(t, half, 1))


def _ret_kernel(q_ref, k_ref, v_ref, gate_ref, cos_ref, sin_ref, lg_ref, gn_ref, o_ref, st_scr):
    @pl.when(pl.program_id(1) == 0)
    def _():
        st_scr[...] = jnp.zeros_like(st_scr)

    def chunk(ci, carry):
        sl = pl.ds(pl.multiple_of(ci * CHUNK, CHUNK), CHUNK)
        cos = cos_ref[sl, :]
        sin = sin_ref[sl, :]
        q = q_ref[0, sl, :].astype(F32)
        k = k_ref[0, sl, :].astype(F32)
        q = q * cos + _swap_halves(q) * sin
        k = (k * cos + _swap_halves(k) * sin) * (LIN_DK ** -0.5)
        la = jnp.broadcast_to(lg_ref[...], (CHUNK, QK_L))
        o = _lin_chunk(q, k, v_ref[0, sl, :], la, st_scr)
        y = jax.nn.silu(gate_ref[0, sl, :].astype(F32)) * _head_rms(o, gn_ref[...])
        o_ref[0, sl, :] = y.astype(BF16)
        return carry

    lax.fori_loop(0, T_LIN // CHUNK, chunk, 0)


def _gla_kernel(q_ref, k_ref, v_ref, gate_ref, ca_ref, a2_ref, ab_ref, gn_ref, o_ref, st_scr):
    @pl.when(pl.program_id(1) == 0)
    def _():
        st_scr[...] = jnp.zeros_like(st_scr)

    def chunk(ci, carry):
        sl = pl.ds(pl.multiple_of(ci * CHUNK, CHUNK), CHUNK)
        z = _dot(ca_ref[0, sl, :].astype(BF16), a2_ref[...]) + ab_ref[...]
        la = (jnp.minimum(z, 0.0) - jnp.log(1.0 + jnp.exp(-jnp.abs(z)))) * (1.0 / GLA_TAU)
        q = q_ref[0, sl, :].astype(F32) * (LIN_DK ** -0.5)
        k = k_ref[0, sl, :].astype(F32)
        o = _lin_chunk(q, k, v_ref[0, sl, :], la, st_scr)
        y = jax.nn.silu(gate_ref[0, sl, :].astype(F32)) * _head_rms(o, gn_ref[...])
        o_ref[0, sl, :] = y.astype(BF16)
        return carry

    lax.fori_loop(0, T_LIN // CHUNK, chunk, 0)


def _lin_specs(off):
    qb, kb = off // QK_L, (off + QK_L) // QK_L
    vb, gb = (off + 2 * QK_L) // V_W, (off + 2 * QK_L + V_W) // V_W
    return [
        pl.BlockSpec((1, T_LIN, QK_L), lambda bi, t: (bi, t, qb)),
        pl.BlockSpec((1, T_LIN, QK_L), lambda bi, t: (bi, t, kb)),
        pl.BlockSpec((1, T_LIN, V_W), lambda bi, t: (bi, t, vb)),
        pl.BlockSpec((1, T_LIN, V_W), lambda bi, t: (bi, t, gb)),
    ]


def _lin_call(kern, name, proj3, extra_specs, extra_args, off):
    b, s, _ = proj3.shape
    return pl.pallas_call(
        kern,
        grid=(b, s // T_LIN),
        in_specs=_lin_specs(off) + extra_specs,
        out_specs=pl.BlockSpec((1, T_LIN, V_W), lambda bi, t: (bi, t, 0)),
        out_shape=jax.ShapeDtypeStruct((b, s, V_W), BF16),
        scratch_shapes=[pltpu.VMEM((HEAD_DV, QK_L), F32)],
        compiler_params=pltpu.CompilerParams(
            dimension_semantics=("parallel", "arbitrary"), vmem_limit_bytes=VMEM_LIMIT),
        name=name,
    )(proj3, proj3, proj3, proj3, *extra_args)


def _retention(proj3, cos, sin, lg, gn):
    extra = [
        pl.BlockSpec((T_LIN, QK_L), lambda bi, t: (t, 0)),
        pl.BlockSpec((T_LIN, QK_L), lambda bi, t: (t, 0)),
        pl.BlockSpec((1, QK_L), lambda bi, t: (0, 0)),
        pl.BlockSpec((1, HEAD_DV), lambda bi, t: (0, 0)),
    ]
    return _lin_call(_ret_kernel, "retention", proj3, extra, (cos, sin, lg, gn), OFF_B)


def _gla(proj3, ca3, a2, ab, gn):
    extra = [
        pl.BlockSpec((1, T_LIN, LANES), lambda bi, t: (bi, t, 0)),
        pl.BlockSpec((LANES, QK_L), lambda bi, t: (0, 0)),
        pl.BlockSpec((1, QK_L), lambda bi, t: (0, 0)),
        pl.BlockSpec((1, HEAD_DV), lambda bi, t: (0, 0)),
    ]
    return _lin_call(_gla_kernel, "gla", proj3, extra, (ca3, a2, ab, gn), OFF_C)


def _merge_kernel(x_ref, g0_ref, g1_ref, g2_ref, ya_ref, yb_ref, yc_ref,
                  wa_ref, wb_ref, wc_ref, wo_ref, o_ref):
    def sig(ref):
        return jax.nn.sigmoid(ref[...].astype(F32))

    merged = (sig(g0_ref) * _dot(ya_ref[...], wa_ref[...])
              + sig(g1_ref) * _dot(yb_ref[...], wb_ref[...])
              + sig(g2_ref) * _dot(yc_ref[...], wc_ref[...]))
    o_ref[...] = x_ref[...] + _dot(merged.astype(BF16), wo_ref[...])


def _merge(x2, proj2, ya, yb, yc, wa, wb, wc, wo):
    n = x2.shape[0]
    row = lambda w: pl.BlockSpec((TM_MERGE, w), lambda i: (i, 0))
    full = lambda a: pl.BlockSpec(a.shape, lambda i: (0, 0))
    return pl.pallas_call(
        _merge_kernel,
        grid=(n // TM_MERGE,),
        in_specs=[
            row(D_MODEL),
            pl.BlockSpec((TM_MERGE, D_MODEL), lambda i: (i, 0)),
            pl.BlockSpec((TM_MERGE, D_MODEL), lambda i: (i, 1)),
            pl.BlockSpec((TM_MERGE, D_MODEL), lambda i: (i, 2)),
            row(V_W), row(V_W), row(V_W),
            full(wa), full(wb), full(wc), full(wo),
        ],
        out_specs=row(D_MODEL),
        out_shape=jax.ShapeDtypeStruct((n, D_MODEL), F32),
        compiler_params=pltpu.CompilerParams(
            dimension_semantics=("parallel",), vmem_limit_bytes=VMEM_LIMIT),
        name="merge",
    )(x2, proj2, proj2, proj2, ya, yb, yc, wa, wb, wc, wo)


def _rope_tables(s):
    half = LIN_DK // 2
    inv = ROPE_BASE ** (-jnp.arange(0, LIN_DK, 2, dtype=F32) / LIN_DK)
    ang = jnp.arange(s, dtype=F32)[:, None] * inv[None, :]
    cos, sin = jnp.cos(ang), jnp.sin(ang)
    del half
    cos_t = jnp.tile(jnp.concatenate([cos, cos], axis=-1), (1, N_HEADS))
    sin_t = jnp.tile(jnp.concatenate([-sin, sin], axis=-1), (1, N_HEADS))
    return cos_t, sin_t


def kernel(x, ffn1_norm, ffn1_w1, ffn1_w3, ffn1_w2, mix_norm, w_in, lam_qk, da_norm, ret_norm,
           gla_a2, gla_a_bias, gla_norm, w_branch_a, w_branch_b, w_branch_c, w_out,
           ffn2_norm, ffn2_w1, ffn2_w3, ffn2_w2, final_norm):
    b, s, d = x.shape
    assert d == D_MODEL and s % T_LIN == 0 and s % TK_ATT == 0 and TK_ATT % TQ_ATT == 0
    assert (b * s) % TM_FFN == 0
    pad_f = D_FF_PAD - D_FF

    def prep_ffn(w1, w3, w2):
        return (jnp.pad(w1, ((0, 0), (0, 0), (0, pad_f))).astype(BF16),
                jnp.pad(w3, ((0, 0), (0, 0), (0, pad_f))).astype(BF16),
                jnp.pad(w2, ((0, 0), (0, pad_f), (0, 0))).astype(BF16))

    f1 = prep_ffn(ffn1_w1, ffn1_w3, ffn1_w2)
    f2 = prep_ffn(ffn2_w1, ffn2_w3, ffn2_w2)

    n_abc = OFF_C + 2 * QK_L + 2 * V_W - GATE_W
    w_main = jnp.concatenate(
        [w_in[:, :, n_abc + GLA_RANK:], w_in[:, :, :n_abc]], axis=-1).astype(BF16)
    w_ca = jnp.pad(w_in[:, :, n_abc:n_abc + GLA_RANK],
                   ((0, 0), (0, 0), (0, LANES - GLA_RANK))).astype(BF16)
    a2 = jnp.pad(gla_a2, ((0, 0), (0, LANES - GLA_RANK), (0, 0))).astype(BF16)
    wa, wb, wc, wo = (w.astype(BF16) for w in (w_branch_a, w_branch_b, w_branch_c, w_out))

    cos_t, sin_t = _rope_tables(s)
    log_g = jnp.log1p(-jnp.exp2(-5.0 - jnp.arange(N_HEADS, dtype=F32)))
    lg = jnp.repeat(log_g, LIN_DK)[None, :]
    fin = final_norm[None, :]

    x2 = x.reshape(b * s, d)
    for l in range(DEPTH):
        lam_init = 0.8 - 0.6 * math.exp(-0.3 * l)
        x2 = _ffn(x2, ffn1_norm[l][None, :], f1[0][l], f1[1][l], f1[2][l], fin, False)
        proj2, ca2 = _inproj(x2, mix_norm[l][None, :], w_main[l], w_ca[l])
        proj3 = proj2.reshape(b, s, PROJ_W)
        ya = _attn(proj3, lam_qk[l], da_norm[l][None, :], lam_init)
        yb = _retention(proj3, cos_t, sin_t, lg, ret_norm[l][None, :])
        yc = _gla(proj3, ca2.reshape(b, s, LANES), a2[l], gla_a_bias[l][None, :], gla_norm[l][None, :])
        x2 = _merge(x2, proj2, ya.reshape(b * s, V_W), yb.reshape(b * s, V_W), yc.reshape(b * s, V_W),
                    wa[l], wb[l], wc[l], wo[l])
        x2 = _ffn(x2, ffn2_norm[l][None, :], f2[0][l], f2[1][l], f2[2][l], fin, l == DEPTH - 1)
    return x2.reshape(b, s, d)
```

```python
import functools
import math

import jax
import jax.numpy as jnp
from jax import lax
from jax.experimental import pallas as pl
from jax.experimental.pallas import tpu as pltpu

F32 = jnp.float32
BF16 = jnp.bfloat16

D_MODEL = 1024
N_HEADS = 4
DEPTH = 4
CHUNK = 64
DA_HEAD = 64
HEAD_DV = 128
LIN_DK = 64
GLA_RANK = 16
GLA_TAU = 16.0
ROPE_BASE = 10000.0
D_FF = 2752
EPS = 1e-6

LANES = 128
MXU_DIM = 256

D_FF_PAD = ((D_FF + MXU_DIM - 1) // MXU_DIM) * MXU_DIM
QK_A = N_HEADS * 2 * DA_HEAD
V_W = N_HEADS * HEAD_DV
QK_L = N_HEADS * LIN_DK
GATE_W = 3 * D_MODEL
OFF_A = GATE_W
OFF_B = OFF_A + 2 * QK_A + V_W
OFF_C = OFF_B + 2 * QK_L + 2 * V_W
PROJ_W = OFF_C + 2 * QK_L + 2 * V_W

TM_FFN = 512
TF_FFN = MXU_DIM
NF_FFN = D_FF_PAD // TF_FFN
TM_PROJ = 512
TN_PROJ = 512
TQ_ATT = 256
TK_ATT = 512
T_LIN = 512
SUB = 16
TM_MERGE = 512
VMEM_LIMIT = 48 * 1024 * 1024


def _rms(xf, g):
    return xf * lax.rsqrt(jnp.mean(xf * xf, axis=-1, keepdims=True) + EPS) * g


def _dot(a, b):
    return jnp.dot(a, b, preferred_element_type=F32)


def _dot_nt(a, b):
    return lax.dot_general(a, b, (((1,), (1,)), ((), ())), preferred_element_type=F32)


def _dot_tn(a, b):
    return lax.dot_general(a, b, (((0,), (0,)), ((), ())), preferred_element_type=F32)


def _ffn_kernel(x_ref, g_ref, w1_ref, w3_ref, w2_ref, gf_ref, o_ref, *, final):
    x = x_ref[...]
    h = _rms(x, g_ref[...]).astype(BF16)
    acc = None
    for c in range(NF_FFN):
        a = _dot(h, w1_ref[c])
        b = _dot(h, w3_ref[c])
        t = (a * jax.nn.sigmoid(a) * b).astype(BF16)
        d = _dot(t, w2_ref[c])
        acc = d if acc is None else acc + d
    y = x + 0.5 * acc
    if final:
        y = _rms(y, gf_ref[...])
    o_ref[...] = y


def _resident(shape):
    return pl.BlockSpec(shape, lambda i: (0,) * len(shape), pipeline_mode=pl.Buffered(1))


def _ffn(x2, g, w1, w3, w2, gf, final):
    n = x2.shape[0]
    return pl.pallas_call(
        functools.partial(_ffn_kernel, final=final),
        grid=(n // TM_FFN,),
        in_specs=[
            pl.BlockSpec((TM_FFN, D_MODEL), lambda i: (i, 0)),
            _resident((1, D_MODEL)),
            _resident(w1.shape), _resident(w3.shape), _resident(w2.shape),
            _resident((1, D_MODEL)),
        ],
        out_specs=pl.BlockSpec((TM_FFN, D_MODEL), lambda i: (i, 0)),
        out_shape=jax.ShapeDtypeStruct((n, D_MODEL), F32),
        compiler_params=pltpu.CompilerParams(
            dimension_semantics=("parallel",), vmem_limit_bytes=VMEM_LIMIT),
        name="ffn",
    )(x2, g, w1, w3, w2, gf)


def _inproj_kernel(x_ref, g_ref, w_ref, wca_ref, proj_ref, ca_ref):
    h = _rms(x_ref[...], g_ref[...]).astype(BF16)
    ca_ref[...] = _dot(h, wca_ref[...])
    for c in range(PROJ_W // TN_PROJ):
        proj_ref[:, c * TN_PROJ:(c + 1) * TN_PROJ] = _dot(h, w_ref[c]).astype(BF16)


def _inproj(x2, g, w, wca):
    n = x2.shape[0]
    return pl.pallas_call(
        _inproj_kernel,
        grid=(n // TM_PROJ,),
        in_specs=[
            pl.BlockSpec((TM_PROJ, D_MODEL), lambda i: (i, 0)),
            _resident((1, D_MODEL)),
            _resident(w.shape),
            _resident(wca.shape),
        ],
        out_specs=[
            pl.BlockSpec((TM_PROJ, PROJ_W), lambda i: (i, 0)),
            pl.BlockSpec((TM_PROJ, LANES), lambda i: (i, 0)),
        ],
        out_shape=[
            jax.ShapeDtypeStruct((n, PROJ_W), BF16),
            jax.ShapeDtypeStruct((n, LANES), F32),
        ],
        compiler_params=pltpu.CompilerParams(
            dimension_semantics=("parallel",), vmem_limit_bytes=VMEM_LIMIT),
        name="inproj",
    )(x2, g, w, wca)


def _attn_kernel(lamqk_ref, g_ref, q_ref, k_ref, v_ref, o_ref,
                 qq_scr, s0_scr, s1_scr, m_scr, l_scr, acc_scr, *, lam_init):
    tq, tk = TQ_ATT, TK_ATT
    qi = pl.program_id(2)
    q = q_ref[0].astype(F32) * (DA_HEAD ** -0.5 * math.log2(math.e))
    lane = lax.broadcasted_iota(jnp.int32, (tq, 2 * DA_HEAD), 1)
    qq_scr[...] = jnp.concatenate(
        [jnp.where(lane < DA_HEAD, q, 0.0), jnp.where(lane >= DA_HEAD, q, 0.0)], axis=0).astype(BF16)

    m_scr[...] = jnp.full(m_scr.shape, -jnp.inf, F32)
    l_scr[...] = jnp.zeros_like(l_scr)
    acc_scr[...] = jnp.zeros_like(acc_scr)

    def scores(j, s_ref):
        start = pl.multiple_of(j * tk, tk)
        s_ref[...] = _dot_nt(qq_scr[...], k_ref[0, pl.ds(start, tk), :])

    def consume(j, s_ref, masked):
        start = pl.multiple_of(j * tk, tk)
        s = s_ref[...]
        if masked:
            key_chunk = (start + lax.broadcasted_iota(jnp.int32, (1, tk), 1)) // CHUNK
            q_chunk = (qi * tq + lax.broadcasted_iota(jnp.int32, (2 * tq, 1), 0) % tq) // CHUNK
            s = jnp.where(key_chunk <= q_chunk, s, -jnp.inf)
        m_prev = m_scr[...]
        m_new = jnp.maximum(m_prev, jnp.max(s, axis=-1, keepdims=True))
        alpha = jnp.exp2(m_prev - m_new)
        p = jnp.exp2(s - jnp.concatenate([m_new] * (tk // LANES), axis=1))
        psum = p[:, 0:LANES]
        for c in range(1, tk // LANES):
            psum = psum + p[:, c * LANES:(c + 1) * LANES]
        l_scr[...] = alpha * l_scr[...] + psum
        acc_scr[...] = alpha * acc_scr[...] + _dot(p.astype(BF16), v_ref[0, pl.ds(start, tk), :])
        m_scr[...] = m_new

    u = (qi * tq) // tk
    scores(0, s0_scr)

    def pair(i, carry):
        scores(2 * i + 1, s1_scr)
        consume(2 * i, s0_scr, False)
        scores(2 * i + 2, s0_scr)
        consume(2 * i + 1, s1_scr, False)
        return carry

    lax.fori_loop(0, u // 2, pair, 0)

    @pl.when(u % 2 == 0)
    def _():
        consume(u, s0_scr, True)

    @pl.when(u % 2 == 1)
    def _():
        scores(u, s1_scr)
        consume(u - 1, s0_scr, False)
        consume(u, s1_scr, True)

    lq = lamqk_ref[...]
    lam = (jnp.exp(jnp.sum(lq[0:1] * lq[1:2], axis=-1, keepdims=True))
           - jnp.exp(jnp.sum(lq[2:3] * lq[3:4], axis=-1, keepdims=True)) + lam_init)
    a = acc_scr[...] / jnp.sum(l_scr[...], axis=-1, keepdims=True)
    out = a[:tq] - lam * a[tq:]
    o_ref[0] = (_rms(out, g_ref[...]) * (1.0 - lam_init)).astype(BF16)


def _attn(proj3, lamqk, g, lam_init):
    b, s, _ = proj3.shape
    qb, kb, vb = OFF_A // LANES, (OFF_A + QK_A) // LANES, (OFF_A + 2 * QK_A) // LANES
    return pl.pallas_call(
        functools.partial(_attn_kernel, lam_init=lam_init),
        grid=(b, N_HEADS, s // TQ_ATT),
        in_specs=[
            pl.BlockSpec((4, DA_HEAD), lambda bi, h, qi: (0, 0)),
            pl.BlockSpec((1, HEAD_DV), lambda bi, h, qi: (0, 0)),
            pl.BlockSpec((1, TQ_ATT, LANES), lambda bi, h, qi: (bi, qi, qb + h)),
            pl.BlockSpec((1, s, LANES), lambda bi, h, qi: (bi, 0, kb + h)),
            pl.BlockSpec((1, s, LANES), lambda bi, h, qi: (bi, 0, vb + h)),
        ],
        out_specs=pl.BlockSpec((1, TQ_ATT, HEAD_DV), lambda bi, h, qi: (bi, qi, h)),
        out_shape=jax.ShapeDtypeStruct((b, s, V_W), BF16),
        scratch_shapes=[
            pltpu.VMEM((2 * TQ_ATT, 2 * DA_HEAD), BF16),
            pltpu.VMEM((2 * TQ_ATT, TK_ATT), F32),
            pltpu.VMEM((2 * TQ_ATT, TK_ATT), F32),
            pltpu.VMEM((2 * TQ_ATT, LANES), F32),
            pltpu.VMEM((2 * TQ_ATT, LANES), F32),
            pltpu.VMEM((2 * TQ_ATT, HEAD_DV), F32),
        ],
        compiler_params=pltpu.CompilerParams(
            dimension_semantics=("parallel", "parallel", "arbitrary"), vmem_limit_bytes=VMEM_LIMIT),
        name="diff_attn",
    )(lamqk, g, proj3, proj3, proj3)


def _stack_heads(t):
    lane = lax.broadcasted_iota(jnp.int32, t.shape, 1)
    return jnp.concatenate(
        [jnp.where((lane // LIN_DK) == h, t, 0.0) for h in range(N_HEADS)], axis=0).astype(BF16)


def _lin_chunk(q, k, v, la, st):
    c = CHUNK
    r = lax.broadcasted_iota(jnp.int32, (c, c), 0)
    cc = lax.broadcasted_iota(jnp.int32, (c, c), 1)
    tri = jnp.where(cc <= r, 1.0, 0.0).astype(BF16)
    la_hi = la.astype(BF16)
    la_lo = (la - la_hi.astype(F32)).astype(BF16)
    cum = _dot(tri, la_hi) + _dot(tri, la_lo)

    rowk = lax.broadcasted_iota(jnp.int32, (c, QK_L), 0)
    srow = lax.broadcasted_iota(jnp.int32, (N_HEADS * SUB, c), 0)
    scol = lax.broadcasted_iota(jnp.int32, (N_HEADS * SUB, c), 1)
    inner = []
    for i in range(c // SUB):
        lo, hi = SUB * i, SUB * (i + 1)
        ref = cum[lo - 1:lo] if i > 0 else jnp.zeros((1, QK_L), F32)
        q_i = q[lo:hi] * jnp.exp(cum[lo:hi] - ref)
        k_i = jnp.where(rowk < hi, k * jnp.exp(ref - cum), 0.0)
        sc = _dot_nt(_stack_heads(q_i), k_i.astype(BF16))
        sc = jnp.where((scol - lo) <= (srow % SUB), sc, 0.0)
        o = _dot(sc.astype(BF16), v)
        inner.append(jnp.concatenate(
            [o[SUB * h:SUB * (h + 1), HEAD_DV * h:HEAD_DV * (h + 1)] for h in range(N_HEADS)], axis=1))
    inner = jnp.concatenate(inner, axis=0)

    cr =_dot_nt(_stack_heads(q * jnp.exp(cum)), st.astype(BF16))
    cross = jnp.concatenate([cr[c * h:c * (h + 1)] for h in range(N_HEADS)], axis=1)
    last = cum[c - 1:c]
    ke = _stack_heads(k * jnp.exp(last - cum))
    vs = jnp.concatenate([v[:, HEAD_DV * h:HEAD_DV * (h + 1)] for h in range(N_HEADS)], axis=0)
    return inner + cross, st * jnp.exp(last) + _dot_tn(vs, ke)


def _head_rms(o, g):
    return jnp.concatenate(
        [_rms(o[:, HEAD_DV * h:HEAD_DV * (h + 1)], g) for h in range(N_HEADS)], axis=1)


def _swap_halves(t):
    lane = lax.broadcasted_iota(jnp.int32, t.shape, 1)
    half = LIN_DK // 2
    return jnp.where((lane % LIN_DK) < half,
                     pltpu.roll(t, QK_L - half, 1), pltpu.roll(t, half, 1))


def _lin_tile(q, k, la, v_ref, gate_ref, gn_ref, o_ref, st_scr):
    st = st_scr[...]
    outs = []
    for ci in range(T_LIN // CHUNK):
        lo, hi = ci * CHUNK, (ci + 1) * CHUNK
        o, st = _lin_chunk(q[lo:hi], k[lo:hi], v_ref[0, lo:hi, :], la[lo:hi], st)
        outs.append(o)
    st_scr[...] = st
    o = jnp.concatenate(outs, axis=0)
    y = jax.nn.silu(gate_ref[0].astype(F32)) * _head_rms(o, gn_ref[...])
    o_ref[0] = y.astype(BF16)


def _ret_kernel(q_ref, k_ref, v_ref, gate_ref, cos_ref, sin_ref, lg_ref, gn_ref, o_ref, st_scr):
    @pl.when(pl.program_id(1) == 0)
    def _():
        st_scr[...] = jnp.zeros_like(st_scr)

    cos = cos_ref[...]
    sin = sin_ref[...]
    q = q_ref[0].astype(F32)
    k = k_ref[0].astype(F32)
    q = q * cos + _swap_halves(q) * sin
    k = (k * cos + _swap_halves(k) * sin) * (LIN_DK ** -0.5)
    la = jnp.broadcast_to(lg_ref[...], (T_LIN, QK_L))
    _lin_tile(q, k, la, v_ref, gate_ref, gn_ref, o_ref, st_scr)


def _gla_kernel(q_ref, k_ref, v_ref, gate_ref, ca_ref, a2_ref, ab_ref, gn_ref, o_ref, st_scr):
    @pl.when(pl.program_id(1) == 0)
    def _():
        st_scr[...] = jnp.zeros_like(st_scr)

    z = _dot(ca_ref[0].astype(BF16), a2_ref[...]) + ab_ref[...]
    la = (jnp.minimum(z, 0.0) - jnp.log(1.0 + jnp.exp(-jnp.abs(z)))) * (1.0 / GLA_TAU)
    q = q_ref[0].astype(F32) * (LIN_DK ** -0.5)
    k = k_ref[0].astype(F32)
    _lin_tile(q, k, la, v_ref, gate_ref, gn_ref, o_ref, st_scr)


def _lin_specs(off):
    qb, kb = off // QK_L, (off + QK_L) // QK_L
    vb, gb = (off + 2 * QK_L) // V_W, (off + 2 * QK_L + V_W) // V_W
    return [
        pl.BlockSpec((1, T_LIN, QK_L), lambda bi, t: (bi, t, qb)),
        pl.BlockSpec((1, T_LIN, QK_L), lambda bi, t: (bi, t, kb)),
        pl.BlockSpec((1, T_LIN, V_W), lambda bi, t: (bi, t, vb)),
        pl.BlockSpec((1, T_LIN, V_W), lambda bi, t: (bi, t, gb)),
    ]


def _lin_call(kern, name, proj3, extra_specs, extra_args, off):
    b, s, _ = proj3.shape
    return pl.pallas_call(
        kern,
        grid=(b, s // T_LIN),
        in_specs=_lin_specs(off) + extra_specs,
        out_specs=pl.BlockSpec((1, T_LIN, V_W), lambda bi, t: (bi, t, 0)),
        out_shape=jax.ShapeDtypeStruct((b, s, V_W), BF16),
        scratch_shapes=[pltpu.VMEM((HEAD_DV, QK_L), F32)],
        compiler_params=pltpu.CompilerParams(
            dimension_semantics=("parallel", "arbitrary"), vmem_limit_bytes=VMEM_LIMIT),
        name=name,
    )(proj3, proj3, proj3, proj3, *extra_args)


def _retention(proj3, cos, sin, lg, gn):
    extra = [
        pl.BlockSpec((T_LIN, QK_L), lambda bi, t: (t, 0)),
        pl.BlockSpec((T_LIN, QK_L), lambda bi, t: (t, 0)),
        pl.BlockSpec((1, QK_L), lambda bi, t: (0, 0)),
        pl.BlockSpec((1, HEAD_DV), lambda bi, t: (0, 0)),
    ]
    return _lin_call(_ret_kernel, "retention", proj3, extra, (cos, sin, lg, gn), OFF_B)


def _gla(proj3, ca3, a2, ab, gn):
    extra = [
        pl.BlockSpec((1, T_LIN, LANES), lambda bi, t: (bi, t, 0)),
        pl.BlockSpec((LANES, QK_L), lambda bi, t: (0, 0)),
        pl.BlockSpec((1, QK_L), lambda bi, t: (0, 0)),
        pl.BlockSpec((1, HEAD_DV), lambda bi, t: (0, 0)),
    ]
    return _lin_call(_gla_kernel, "gla", proj3, extra, (ca3, a2, ab, gn), OFF_C)


def _merge_kernel(x_ref, g0_ref, g1_ref, g2_ref, ya_ref, yb_ref, yc_ref,
                  wa_ref, wb_ref, wc_ref, wo_ref, o_ref):
    def sig(ref):
        return jax.nn.sigmoid(ref[...].astype(F32))

    merged = (sig(g0_ref) * _dot(ya_ref[...], wa_ref[...])
              + sig(g1_ref) * _dot(yb_ref[...], wb_ref[...])
              + sig(g2_ref) * _dot(yc_ref[...], wc_ref[...]))
    o_ref[...] = x_ref[...] + _dot(merged.astype(BF16), wo_ref[...])


def _merge(x2, proj2, ya, yb, yc, wa, wb, wc, wo):
    n = x2.shape[0]
    row = lambda w: pl.BlockSpec((TM_MERGE, w), lambda i: (i, 0))
    full = lambda a: pl.BlockSpec(a.shape, lambda i: (0, 0))
    return pl.pallas_call(
        _merge_kernel,
        grid=(n // TM_MERGE,),
        in_specs=[
            row(D_MODEL),
            pl.BlockSpec((TM_MERGE, D_MODEL), lambda i: (i, 0)),
            pl.BlockSpec((TM_MERGE, D_MODEL), lambda i: (i, 1)),
            pl.BlockSpec((TM_MERGE, D_MODEL), lambda i: (i, 2)),
            row(V_W), row(V_W), row(V_W),
            full(wa), full(wb), full(wc), full(wo),
        ],
        out_specs=row(D_MODEL),
        out_shape=jax.ShapeDtypeStruct((n, D_MODEL), F32),
        compiler_params=pltpu.CompilerParams(
            dimension_semantics=("parallel",), vmem_limit_bytes=VMEM_LIMIT),
        name="merge",
    )(x2, proj2, proj2, proj2, ya, yb, yc, wa, wb, wc, wo)


def _rope_tables(s):
    half = LIN_DK // 2
    inv = ROPE_BASE ** (-jnp.arange(0, LIN_DK, 2, dtype=F32) / LIN_DK)
    ang = jnp.arange(s, dtype=F32)[:, None] * inv[None, :]
    cos, sin = jnp.cos(ang), jnp.sin(ang)
    del half
    cos_t = jnp.tile(jnp.concatenate([cos, cos], axis=-1), (1, N_HEADS))
    sin_t = jnp.tile(jnp.concatenate([-sin, sin], axis=-1), (1, N_HEADS))
    return cos_t, sin_t


def kernel(x, ffn1_norm, ffn1_w1, ffn1_w3, ffn1_w2, mix_norm, w_in, lam_qk, da_norm, ret_norm,
           gla_a2, gla_a_bias, gla_norm, w_branch_a, w_branch_b, w_branch_c, w_out,
           ffn2_norm, ffn2_w1, ffn2_w3, ffn2_w2, final_norm):
    b, s, d = x.shape
    assert d == D_MODEL and s % T_LIN == 0 and s % TK_ATT == 0 and TK_ATT % TQ_ATT == 0
    assert (b * s) % TM_FFN == 0
    pad_f = D_FF_PAD - D_FF

    def prep_ffn(w1, w3, w2):
        def cols(w):
            w = jnp.pad(w, ((0, 0), (0, 0), (0, pad_f))).astype(BF16)
            return w.reshape(DEPTH, D_MODEL, NF_FFN, TF_FFN).transpose(0, 2, 1, 3)
        w2 = jnp.pad(w2, ((0, 0), (0, pad_f), (0, 0))).astype(BF16)
        return cols(w1), cols(w3), w2.reshape(DEPTH, NF_FFN, TF_FFN, D_MODEL)

    f1 = prep_ffn(ffn1_w1, ffn1_w3, ffn1_w2)
    f2 = prep_ffn(ffn2_w1, ffn2_w3, ffn2_w2)

    n_abc = OFF_C + 2 * QK_L + 2 * V_W - GATE_W
    w_main = jnp.concatenate(
        [w_in[:, :, n_abc + GLA_RANK:], w_in[:, :, :n_abc]], axis=-1).astype(BF16)
    w_main = w_main.reshape(DEPTH, D_MODEL, PROJ_W // TN_PROJ, TN_PROJ).transpose(0, 2, 1, 3)
    w_ca = jnp.pad(w_in[:, :, n_abc:n_abc + GLA_RANK],
                   ((0, 0), (0, 0), (0, LANES - GLA_RANK))).astype(BF16)
    a2 = jnp.pad(gla_a2, ((0, 0), (0, LANES - GLA_RANK), (0, 0))).astype(BF16)
    wa, wb, wc, wo = (w.astype(BF16) for w in (w_branch_a, w_branch_b, w_branch_c, w_out))

    cos_t, sin_t = _rope_tables(s)
    log_g = jnp.log1p(-jnp.exp2(-5.0 - jnp.arange(N_HEADS, dtype=F32)))
    lg = jnp.repeat(log_g, LIN_DK)[None, :]
    fin = final_norm[None, :]

    x2 = x.reshape(b * s, d)
    for l in range(DEPTH):
        lam_init = 0.8 - 0.6 * math.exp(-0.3 * l)
        x2 = _ffn(x2, ffn1_norm[l][None, :], f1[0][l], f1[1][l], f1[2][l], fin, False)
        proj2, ca2 = _inproj(x2, mix_norm[l][None, :], w_main[l], w_ca[l])
        proj3 = proj2.reshape(b, s, PROJ_W)
        ya = _attn(proj3, lam_qk[l], da_norm[l][None, :], lam_init)
        yb = _retention(proj3, cos_t, sin_t, lg, ret_norm[l][None, :])
        yc = _gla(proj3, ca2.reshape(b, s, LANES), a2[l], gla_a_bias[l][None, :], gla_norm[l][None, :])
        x2 = _merge(x2, proj2, ya.reshape(b * s, V_W), yb.reshape(b * s, V_W), yc.reshape(b * s, V_W),
                    wa[l], wb[l], wc[l], wo[l])
        x2 = _ffn(x2, ffn2_norm[l][None, :], f2[0][l], f2[1][l], f2[2][l], fin, l == DEPTH - 1)
    return x2.reshape(b, s, d)
```

```python
import functools
import math

import jax
import jax.numpy as jnp
from jax import lax
from jax.experimental import pallas as pl
from jax.experimental.pallas import tpu as pltpu

F32 = jnp.float32
BF16 = jnp.bfloat16

D_MODEL = 1024
N_HEADS = 4
DEPTH = 4
CHUNK = 64
DA_HEAD = 64
HEAD_DV = 128
LIN_DK = 64
GLA_RANK = 16
GLA_TAU = 16.0
ROPE_BASE = 10000.0
D_FF = 2752
EPS = 1e-6

LANES = 128
MXU_DIM = 256

D_FF_PAD = ((D_FF + MXU_DIM - 1) // MXU_DIM) * MXU_DIM
QK_A = N_HEADS * 2 * DA_HEAD
V_W = N_HEADS * HEAD_DV
QK_L = N_HEADS * LIN_DK
GATE_W = 3 * D_MODEL
OFF_A = GATE_W
OFF_B = OFF_A + 2 * QK_A + V_W
OFF_C = OFF_B + 2 * QK_L + 2 * V_W
PROJ_W = OFF_C + 2 * QK_L + 2 * V_W

TM_FFN = 512
TF_FFN = MXU_DIM
NF_FFN = D_FF_PAD // TF_FFN
TM_PROJ = 512
TN_PROJ = 512
TQ_ATT = 512
TK_ATT = 512
T_LIN = 512
SUB = 16
TM_MERGE = 512
VMEM_LIMIT = 48 * 1024 * 1024


def _rms(xf, g):
    return xf * lax.rsqrt(jnp.mean(xf * xf, axis=-1, keepdims=True) + EPS) * g


def _dot(a, b):
    return jnp.dot(a, b, preferred_element_type=F32)


def _dot_nt(a, b):
    return lax.dot_general(a, b, (((1,), (1,)), ((), ())), preferred_element_type=F32)


def _dot_tn(a, b):
    return lax.dot_general(a, b, (((0,), (0,)), ((), ())), preferred_element_type=F32)


def _ffn_kernel(x_ref, g_ref, w1_ref, w3_ref, w2_ref, gf_ref, o_ref, *, final):
    x = x_ref[...]
    h = _rms(x, g_ref[...]).astype(BF16)
    acc = None
    for c in range(NF_FFN):
        a = _dot(h, w1_ref[c])
        b = _dot(h, w3_ref[c])
        t = (a * jax.nn.sigmoid(a) * b).astype(BF16)
        d = _dot(t, w2_ref[c])
        acc = d if acc is None else acc + d
    y = x + 0.5 * acc
    if final:
        y = _rms(y, gf_ref[...])
    o_ref[...] = y


def _resident(shape):
    return pl.BlockSpec(shape, lambda i: (0,) * len(shape), pipeline_mode=pl.Buffered(1))


def _ffn(x2, g, w1, w3, w2, gf, final):
    n = x2.shape[0]
    return pl.pallas_call(
        functools.partial(_ffn_kernel, final=final),
        grid=(n // TM_FFN,),
        in_specs=[
            pl.BlockSpec((TM_FFN, D_MODEL), lambda i: (i, 0)),
            _resident((1, D_MODEL)),
            _resident(w1.shape), _resident(w3.shape), _resident(w2.shape),
            _resident((1, D_MODEL)),
        ],
        out_specs=pl.BlockSpec((TM_FFN, D_MODEL), lambda i: (i, 0)),
        out_shape=jax.ShapeDtypeStruct((n, D_MODEL), F32),
        compiler_params=pltpu.CompilerParams(
            dimension_semantics=("parallel",), vmem_limit_bytes=VMEM_LIMIT),
        name="ffn",
    )(x2, g, w1, w3, w2, gf)


def _inproj_kernel(x_ref, g_ref, w_ref, wca_ref, proj_ref, ca_ref):
    h = _rms(x_ref[...], g_ref[...]).astype(BF16)
    ca_ref[...] = _dot(h, wca_ref[...])
    for c in range(PROJ_W // TN_PROJ):
        proj_ref[:, c * TN_PROJ:(c + 1) * TN_PROJ] = _dot(h, w_ref[c]).astype(BF16)


def _inproj(x2, g, w, wca):
    n = x2.shape[0]
    return pl.pallas_call(
        _inproj_kernel,
        grid=(n // TM_PROJ,),
        in_specs=[
            pl.BlockSpec((TM_PROJ, D_MODEL), lambda i: (i, 0)),
            _resident((1, D_MODEL)),
            _resident(w.shape),
            _resident(wca.shape),
        ],
        out_specs=[
            pl.BlockSpec((TM_PROJ, PROJ_W), lambda i: (i, 0)),
            pl.BlockSpec((TM_PROJ, LANES), lambda i: (i, 0)),
        ],
        out_shape=[
            jax.ShapeDtypeStruct((n, PROJ_W), BF16),
            jax.ShapeDtypeStruct((n, LANES), F32),
        ],
        compiler_params=pltpu.CompilerParams(
            dimension_semantics=("parallel",), vmem_limit_bytes=VMEM_LIMIT),
        name="inproj",
    )(x2, g, w, wca)


def _attn_kernel(lamqk_ref, g_ref, q_ref, k_ref, v_ref, o_ref,
                 qq_scr, vt_scr, s0_scr, s1_scr, m_scr, l_scr, acc_scr, *, lam_init):
    tq, tk = TQ_ATT, TK_ATT
    qi = pl.program_id(2)

    @pl.when(qi == 0)
    def _():
        for t in range(vt_scr.shape[0]):
            vt_scr[t] = v_ref[0, t * tk:(t + 1) * tk, :].T

    q = q_ref[0].astype(F32) * (DA_HEAD ** -0.5 * math.log2(math.e))
    lane = lax.broadcasted_iota(jnp.int32, (tq, 2 * DA_HEAD), 1)
    qq_scr[...] = jnp.concatenate(
        [jnp.where(lane < DA_HEAD, q, 0.0), jnp.where(lane >= DA_HEAD, q, 0.0)], axis=0).astype(BF16)

    m_scr[...] = jnp.full(m_scr.shape, -jnp.inf, F32)
    l_scr[...] = jnp.zeros_like(l_scr)
    acc_scr[...] = jnp.zeros_like(acc_scr)

    def scores(j, s_ref):
        start = pl.multiple_of(j * tk, tk)
        s_ref[...] = _dot_nt(k_ref[0, pl.ds(start, tk), :], qq_scr[...])

    def consume(j, s_ref, masked):
        s = s_ref[...]
        if masked:
            key_chunk = (j * tk + lax.broadcasted_iota(jnp.int32, (tk, 1), 0)) // CHUNK
            q_chunk = (qi * tq + lax.broadcasted_iota(jnp.int32, (1, 2 * tq), 1) % tq) // CHUNK
            s = jnp.where(key_chunk <= q_chunk, s, -jnp.inf)
        m_prev = m_scr[...]
        m_new = jnp.maximum(m_prev, jnp.max(s, axis=0, keepdims=True))
        alpha = jnp.exp2(m_prev - m_new)
        p = jnp.exp2(s - m_new)
        l_scr[...] = alpha * l_scr[...] + jnp.sum(p, axis=0, keepdims=True)
        acc_scr[...] = alpha * acc_scr[...] + _dot(vt_scr[j], p.astype(BF16))
        m_scr[...] = m_new

    u = (qi * tq) // tk
    scores(0, s0_scr)

    def pair(i, carry):
        scores(2 * i + 1, s1_scr)
        consume(2 * i, s0_scr, False)
        scores(2 * i + 2, s0_scr)
        consume(2 * i + 1, s1_scr, False)
        return carry

    lax.fori_loop(0, u // 2, pair, 0)

    @pl.when(u % 2 == 0)
    def _():
        consume(u, s0_scr, True)

    @pl.when(u % 2 == 1)
    def _():
        scores(u, s1_scr)
        consume(u - 1, s0_scr, False)
        consume(u, s1_scr, True)

    lq = lamqk_ref[...]
    lam = (jnp.exp(jnp.sum(lq[0:1] * lq[1:2], axis=-1, keepdims=True))
           - jnp.exp(jnp.sum(lq[2:3] * lq[3:4], axis=-1, keepdims=True)) + lam_init)
    a = acc_scr[...] / l_scr[...]
    out = (a[:, :tq] - lam * a[:, tq:]).T
    o_ref[0] = (_rms(out, g_ref[...]) * (1.0 - lam_init)).astype(BF16)


def _attn(proj3, lamqk, g, lam_init):
    b, s, _ = proj3.shape
    qb, kb, vb = OFF_A // LANES, (OFF_A + QK_A) // LANES, (OFF_A + 2 * QK_A) // LANES
    return pl.pallas_call(
        functools.partial(_attn_kernel, lam_init=lam_init),
        grid=(b, N_HEADS, s // TQ_ATT),
        in_specs=[
            pl.BlockSpec((4, DA_HEAD), lambda bi, h, qi: (0, 0)),
            pl.BlockSpec((1, HEAD_DV), lambda bi, h, qi: (0, 0)),
            pl.BlockSpec((1, TQ_ATT, LANES), lambda bi, h, qi: (bi, qi, qb + h)),
            pl.BlockSpec((1, s, LANES), lambda bi, h, qi: (bi, 0, kb + h)),
            pl.BlockSpec((1, s, LANES), lambda bi, h, qi: (bi, 0, vb + h)),
        ],
        out_specs=pl.BlockSpec((1, TQ_ATT, HEAD_DV), lambda bi, h, qi: (bi, qi, h)),
        out_shape=jax.ShapeDtypeStruct((b, s, V_W), BF16),
        scratch_shapes=[
            pltpu.VMEM((2 * TQ_ATT, 2 * DA_HEAD), BF16),
            pltpu.VMEM((s // TK_ATT, HEAD_DV, TK_ATT), BF16),
            pltpu.VMEM((TK_ATT, 2 * TQ_ATT), F32),
            pltpu.VMEM((TK_ATT, 2 * TQ_ATT), F32),
            pltpu.VMEM((1, 2 * TQ_ATT), F32),
            pltpu.VMEM((1, 2 * TQ_ATT), F32),
            pltpu.VMEM((HEAD_DV, 2 * TQ_ATT), F32),
        ],
        compiler_params=pltpu.CompilerParams(
            dimension_semantics=("parallel", "parallel", "arbitrary"), vmem_limit_bytes=VMEM_LIMIT),
        name="diff_attn",
    )(lamqk, g, proj3, proj3, proj3)


def _stack_heads(t):
    lane = lax.broadcasted_iota(jnp.int32, t.shape, 1)
    return jnp.concatenate(
        [jnp.where((lane // LIN_DK) % N_HEADS == h, t, 0.0) for h in range(N_HEADS)],
        axis=0).astype(BF16)


def _head_rms(o, g):
    return jnp.concatenate(
        [_rms(o[:, HEAD_DV * h:HEAD_DV * (h + 1)], g) for h in range(N_HEADS)], axis=1)


def _swap_halves(t):
    lane = lax.broadcasted_iota(jnp.int32, t.shape, 1)
    half = LIN_DK // 2
    return jnp.where((lane % LIN_DK) < half,
                     pltpu.roll(t, QK_L - half, 1), pltpu.roll(t, half, 1))


def _lin_tile(q, k, la, v_ref, gate_ref, gn_ref, o_ref, st_scr):
    c = CHUNK
    n_chunks = T_LIN // c
    sl = [slice(i * c, (i + 1) * c) for i in range(n_chunks)]
    r = lax.broadcasted_iota(jnp.int32, (c, c), 0)
    cc = lax.broadcasted_iota(jnp.int32, (c, c), 1)
    tri = jnp.where(cc <= r, 1.0, 0.0).astype(BF16)
    cum = []
    for s in sl:
        la_hi = la[s].astype(BF16)
        la_lo = (la[s] - la_hi.astype(F32)).astype(BF16)
        cum.append(_dot(tri, la_hi) + _dot(tri, la_lo))

    row_blk = lax.broadcasted_iota(jnp.int32, (c, QK_L), 0) // SUB
    qb, kb = [], []
    for i, s in enumerate(sl):
        q_parts, k_parts = [], []
        for j in range(c // SUB):
            ref = cum[i][SUB * (j + 1) - 1:SUB * (j + 1)]
            q_parts.append(jnp.where(row_blk >= j, q[s] * jnp.exp(cum[i] - ref), 0.0))
            k_parts.append(jnp.where(row_blk == j, k[s] * jnp.exp(ref - cum[i]), 0.0))
        qb.append(jnp.concatenate(q_parts, axis=1).astype(BF16))
        kb.append(_stack_heads(jnp.concatenate(k_parts, axis=1)))

    srow = lax.broadcasted_iota(jnp.int32, (c, N_HEADS * c), 0)
    scol = lax.broadcasted_iota(jnp.int32, (c, N_HEADS * c), 1)
    causal = scol % c <= srow
    sc = [jnp.where(causal, _dot_nt(qb[i], kb[i]), 0.0).astype(BF16) for i in range(n_chunks)]

    vlane = lax.broadcasted_iota(jnp.int32, (1, V_W), 1) // HEAD_DV
    vmask = [jnp.where(vlane == h, 1.0, 0.0).astype(BF16) for h in range(N_HEADS)]
    v = [v_ref[0, s, :] for s in sl]
    inner = [_dot(sc[i], jnp.concatenate([v[i] * vmask[h] for h in range(N_HEADS)], axis=0))
             for i in range(n_chunks)]

    last = [cu[c - 1:c] for cu in cum]
    upd = []
    for i, s in enumerate(sl):
        ke = _stack_heads(k[s] * jnp.exp(last[i] - cum[i]))
        vs = jnp.concatenate(
            [v[i][:, HEAD_DV * h:HEAD_DV * (h + 1)] for h in range(N_HEADS)], axis=0)
        upd.append(_dot_tn(vs, ke))

    st = st_scr[...]
    slane = lax.broadcasted_iota(jnp.int32, st.shape, 1) // LIN_DK
    outs = []
    for i, s in enumerate(sl):
        st_bd = jnp.concatenate(
            [jnp.where(slane == h, st, 0.0) for h in range(N_HEADS)], axis=0).astype(BF16)
        cross = _dot_nt((q[s] * jnp.exp(cum[i])).astype(BF16), st_bd)
        outs.append(inner[i] + cross)
        st = st * jnp.exp(last[i]) + upd[i]
    st_scr[...] = st
    o = jnp.concatenate(outs, axis=0)
    y = jax.nn.silu(gate_ref[0].astype(F32)) * _head_rms(o, gn_ref[...])
    o_ref[0] = y.astype(BF16)


def _ret_kernel(q_ref, k_ref, v_ref, gate_ref, cos_ref, sin_ref, lg_ref, gn_ref, o_ref, st_scr):
    @pl.when(pl.program_id(1) == 0)
    def _():
        st_scr[...] = jnp.zeros_like(st_scr)

    cos = cos_ref[...]
    sin = sin_ref[...]
    q = q_ref[0].astype(F32)
    k = k_ref[0].astype(F32)
    q = q * cos + _swap_halves(q) * sin
    k = (k * cos + _swap_halves(k) * sin) * (LIN_DK ** -0.5)
    la = jnp.broadcast_to(lg_ref[...], (T_LIN, QK_L))
    _lin_tile(q, k, la, v_ref, gate_ref, gn_ref, o_ref, st_scr)


def _gla_kernel(q_ref, k_ref, v_ref, gate_ref, ca_ref, a2_ref, ab_ref, gn_ref, o_ref, st_scr):
    @pl.when(pl.program_id(1) == 0)
    def _():
        st_scr[...] = jnp.zeros_like(st_scr)

    z = _dot(ca_ref[0].astype(BF16), a2_ref[...]) + ab_ref[...]
    la = (jnp.minimum(z, 0.0) - jnp.log(1.0 + jnp.exp(-jnp.abs(z)))) * (1.0 / GLA_TAU)
    q = q_ref[0].astype(F32) * (LIN_DK ** -0.5)
    k = k_ref[0].astype(F32)
    _lin_tile(q, k, la, v_ref, gate_ref, gn_ref, o_ref, st_scr)


def _lin_specs(off):
    qb, kb = off // QK_L, (off + QK_L) // QK_L
    vb, gb = (off + 2 * QK_L) // V_W, (off + 2 * QK_L + V_W) // V_W
    return [
        pl.BlockSpec((1, T_LIN, QK_L), lambda bi, t: (bi, t, qb)),
        pl.BlockSpec((1, T_LIN, QK_L), lambda bi, t: (bi, t, kb)),
        pl.BlockSpec((1, T_LIN, V_W), lambda bi, t: (bi, t, vb)),
        pl.BlockSpec((1, T_LIN, V_W), lambda bi, t: (bi, t, gb)),
    ]


def _lin_call(kern, name, proj3, extra_specs, extra_args, off):
    b, s, _ = proj3.shape
    return pl.pallas_call(
        kern,
        grid=(b, s // T_LIN),
        in_specs=_lin_specs(off) + extra_specs,
        out_specs=pl.BlockSpec((1, T_LIN, V_W), lambda bi, t: (bi, t, 0)),
        out_shape=jax.ShapeDtypeStruct((b, s, V_W), BF16),
        scratch_shapes=[pltpu.VMEM((HEAD_DV, QK_L), F32)],
        compiler_params=pltpu.CompilerParams(
            dimension_semantics=("parallel", "arbitrary"), vmem_limit_bytes=VMEM_LIMIT),
        name=name,
    )(proj3, proj3, proj3, proj3, *extra_args)


def _retention(proj3, cos, sin, lg, gn):
    extra = [
        pl.BlockSpec((T_LIN, QK_L), lambda bi, t: (t, 0)),
        pl.BlockSpec((T_LIN, QK_L), lambda bi, t: (t, 0)),
        pl.BlockSpec((1, QK_L), lambda bi, t: (0, 0)),
        pl.BlockSpec((1, HEAD_DV), lambda bi, t: (0, 0)),
    ]
    return _lin_call(_ret_kernel, "retention", proj3, extra, (cos, sin, lg, gn), OFF_B)


def _gla(proj3, ca3, a2, ab, gn):
    extra = [
        pl.BlockSpec((1, T_LIN, LANES), lambda bi, t: (bi, t, 0)),
        pl.BlockSpec((LANES, QK_L), lambda bi, t: (0, 0)),
        pl.BlockSpec((1, QK_L), lambda bi, t: (0, 0)),
        pl.BlockSpec((1, HEAD_DV), lambda bi, t: (0, 0)),
    ]
    return _lin_call(_gla_kernel, "gla", proj3, extra, (ca3, a2, ab, gn), OFF_C)


def _merge_kernel(x_ref, g0_ref, g1_ref, g2_ref, ya_ref, yb_ref, yc_ref,
                  wa_ref, wb_ref, wc_ref, wo_ref, o_ref):
    def sig(ref):
        return jax.nn.sigmoid(ref[...].astype(F32))

    merged = (sig(g0_ref) * _dot(ya_ref[...], wa_ref[...])
              + sig(g1_ref) * _dot(yb_ref[...], wb_ref[...])
              + sig(g2_ref) * _dot(yc_ref[...], wc_ref[...]))
    o_ref[...] = x_ref[...] + _dot(merged.astype(BF16), wo_ref[...])


def _merge(x2, proj2, ya, yb, yc, wa, wb, wc, wo):
    n = x2.shape[0]
    row = lambda w: pl.BlockSpec((TM_MERGE, w), lambda i: (i, 0))
    full = lambda a: pl.BlockSpec(a.shape, lambda i: (0, 0))
    return pl.pallas_call(
        _merge_kernel,
        grid=(n // TM_MERGE,),
        in_specs=[
            row(D_MODEL),
            pl.BlockSpec((TM_MERGE, D_MODEL), lambda i: (i, 0)),
            pl.BlockSpec((TM_MERGE, D_MODEL), lambda i: (i, 1)),
            pl.BlockSpec((TM_MERGE, D_MODEL), lambda i: (i, 2)),
            row(V_W), row(V_W), row(V_W),
            full(wa), full(wb), full(wc), full(wo),
        ],
        out_specs=row(D_MODEL),
        out_shape=jax.ShapeDtypeStruct((n, D_MODEL), F32),
        compiler_params=pltpu.CompilerParams(
            dimension_semantics=("parallel",), vmem_limit_bytes=VMEM_LIMIT),
        name="merge",
    )(x2, proj2, proj2, proj2, ya, yb, yc, wa, wb, wc, wo)


def _rope_tables(s):
    half = LIN_DK // 2
    inv = ROPE_BASE ** (-jnp.arange(0, LIN_DK, 2, dtype=F32) / LIN_DK)
    ang = jnp.arange(s, dtype=F32)[:, None] * inv[None, :]
    cos, sin = jnp.cos(ang), jnp.sin(ang)
    del half
    cos_t = jnp.tile(jnp.concatenate([cos, cos], axis=-1), (1, N_HEADS))
    sin_t = jnp.tile(jnp.concatenate([-sin, sin], axis=-1), (1, N_HEADS))
    return cos_t, sin_t


def kernel(x, ffn1_norm, ffn1_w1, ffn1_w3, ffn1_w2, mix_norm, w_in, lam_qk, da_norm, ret_norm,
           gla_a2, gla_a_bias, gla_norm, w_branch_a, w_branch_b, w_branch_c, w_out,
           ffn2_norm, ffn2_w1, ffn2_w3, ffn2_w2, final_norm):
    b, s, d = x.shape
    assert d == D_MODEL and s % T_LIN == 0 and s % TK_ATT == 0 and TK_ATT % TQ_ATT == 0
    assert (b * s) % TM_FFN == 0
    pad_f = D_FF_PAD - D_FF

    def prep_ffn(w1, w3, w2):
        def cols(w):
            w = jnp.pad(w, ((0, 0), (0, 0), (0, pad_f))).astype(BF16)
            return w.reshape(DEPTH, D_MODEL, NF_FFN, TF_FFN).transpose(0, 2, 1, 3)
        w2 = jnp.pad(w2, ((0, 0), (0, pad_f), (0, 0))).astype(BF16)
        return cols(w1), cols(w3), w2.reshape(DEPTH, NF_FFN, TF_FFN, D_MODEL)

    f1 = prep_ffn(ffn1_w1, ffn1_w3, ffn1_w2)
    f2 = prep_ffn(ffn2_w1, ffn2_w3, ffn2_w2)

    n_abc = OFF_C + 2 * QK_L + 2 * V_W - GATE_W
    w_main = jnp.concatenate(
        [w_in[:, :, n_abc + GLA_RANK:], w_in[:, :, :n_abc]], axis=-1).astype(BF16)
    w_main = w_main.reshape(DEPTH, D_MODEL, PROJ_W // TN_PROJ, TN_PROJ).transpose(0, 2, 1, 3)
    w_ca = jnp.pad(w_in[:, :, n_abc:n_abc + GLA_RANK],
                   ((0, 0), (0, 0), (0, LANES - GLA_RANK))).astype(BF16)
    a2 = jnp.pad(gla_a2, ((0, 0), (0, LANES - GLA_RANK), (0, 0))).astype(BF16)
    wa, wb, wc, wo = (w.astype(BF16) for w in (w_branch_a, w_branch_b, w_branch_c, w_out))

    cos_t, sin_t = _rope_tables(s)
    log_g = jnp.log1p(-jnp.exp2(-5.0 - jnp.arange(N_HEADS, dtype=F32)))
    lg = jnp.repeat(log_g, LIN_DK)[None, :]
    fin = final_norm[None, :]

    x2 = x.reshape(b * s, d)
    for l in range(DEPTH):
        lam_init = 0.8 - 0.6 * math.exp(-0.3 * l)
        x2 = _ffn(x2, ffn1_norm[l][None, :], f1[0][l], f1[1][l], f1[2][l], fin, False)
        proj2, ca2 = _inproj(x2, mix_norm[l][None, :], w_main[l], w_ca[l])
        proj3 = proj2.reshape(b, s, PROJ_W)
        ya = _attn(proj3, lam_qk[l], da_norm[l][None, :], lam_init)
        yb = _retention(proj3, cos_t, sin_t, lg, ret_norm[l][None, :])
        yc = _gla(proj3, ca2.reshape(b, s, LANES), a2[l], gla_a_bias[l][None, :], gla_norm[l][None, :])
        x2 = _merge(x2, proj2, ya.reshape(b * s, V_W), yb.reshape(b * s, V_W), yc.reshape(b * s, V_W),
                    wa[l], wb[l], wc[l], wo[l])
        x2 = _ffn(x2, ffn2_norm[l][None, :], f2[0][l], f2[1][l], f2[2][l], fin, l == DEPTH - 1)
    return x2.reshape(b, s, d)
```

```python
import functools
import math

import jax
import jax.numpy as jnp
from jax import lax
from jax.experimental import pallas as pl
from jax.experimental.pallas import tpu as pltpu

F32 = jnp.float32
BF16 = jnp.bfloat16

D_MODEL = 1024
N_HEADS = 4
DEPTH = 4
CHUNK = 64
DA_HEAD = 64
HEAD_DV = 128
LIN_DK = 64
GLA_RANK = 16
GLA_TAU = 16.0
ROPE_BASE = 10000.0
D_FF = 2752
EPS = 1e-6

LANES = 128
MXU_DIM = 256

D_FF_PAD = ((D_FF + MXU_DIM - 1) // MXU_DIM) * MXU_DIM
QK_A = N_HEADS * 2 * DA_HEAD
V_W = N_HEADS * HEAD_DV
QK_L = N_HEADS * LIN_DK
GATE_W = 3 * D_MODEL
OFF_A = GATE_W
OFF_B = OFF_A + 2 * QK_A + V_W
OFF_C = OFF_B + 2 * QK_L + 2 * V_W
PROJ_W = OFF_C + 2 * QK_L + 2 * V_W

TM_FFN = 512
TF_FFN = MXU_DIM
NF_FFN = D_FF_PAD // TF_FFN
TM_PROJ = 512
TN_PROJ = 512
TQ_ATT = 512
TK_ATT = 512
VT_ROWS = HEAD_DV + 16
T_LIN = 1024
SUB = 16
TM_MERGE = 512
VMEM_LIMIT = 48 * 1024 * 1024


def _rms(xf, g):
    return xf * lax.rsqrt(jnp.mean(xf * xf, axis=-1, keepdims=True) + EPS) * g


def _dot(a, b):
    return jnp.dot(a, b, preferred_element_type=F32)


def _dot_nt(a, b):
    return lax.dot_general(a, b, (((1,), (1,)), ((), ())), preferred_element_type=F32)


def _dot_tn(a, b):
    return lax.dot_general(a, b, (((0,), (0,)), ((), ())), preferred_element_type=F32)


def _ffn_kernel(x_ref, g_ref, w1_ref, w3_ref, w2_ref, gf_ref, o_ref, *, final):
    x = x_ref[...]
    h = _rms(x, g_ref[...]).astype(BF16)
    acc = None
    for c in range(NF_FFN):
        cs = slice(c * TF_FFN, (c + 1) * TF_FFN)
        a = _dot(h, w1_ref[:, cs])
        b = _dot(h, w3_ref[:, cs])
        t = (a * jax.nn.sigmoid(a) * b).astype(BF16)
        d = _dot(t, w2_ref[cs, :])
        acc = d if acc is None else acc + d
    y = x + 0.5 * acc
    if final:
        y = _rms(y, gf_ref[...])
    o_ref[...] = y


def _resident(shape):
    return pl.BlockSpec(shape, lambda i: (0,) * len(shape), pipeline_mode=pl.Buffered(1))


def _ffn(x2, g, w1, w3, w2, gf, final):
    n = x2.shape[0]
    return pl.pallas_call(
        functools.partial(_ffn_kernel, final=final),
        grid=(n // TM_FFN,),
        in_specs=[
            pl.BlockSpec((TM_FFN, D_MODEL), lambda i: (i, 0)),
            _resident((1, D_MODEL)),
            _resident(w1.shape), _resident(w3.shape), _resident(w2.shape),
            _resident((1, D_MODEL)),
        ],
        out_specs=pl.BlockSpec((TM_FFN, D_MODEL), lambda i: (i, 0)),
        out_shape=jax.ShapeDtypeStruct((n, D_MODEL), F32),
        compiler_params=pltpu.CompilerParams(
            dimension_semantics=("parallel",), vmem_limit_bytes=VMEM_LIMIT),
        name="ffn",
    )(x2, g, w1, w3, w2, gf)


def _inproj_kernel(x_ref, g_ref, w_ref, wca_ref, proj_ref, ca_ref):
    h = _rms(x_ref[...], g_ref[...]).astype(BF16)
    ca_ref[...] = _dot(h, wca_ref[...])
    for c in range(PROJ_W // TN_PROJ):
        cs = slice(c * TN_PROJ, (c + 1) * TN_PROJ)
        proj_ref[:, cs] = _dot(h, w_ref[:, cs]).astype(BF16)


def _inproj(x2, g, w, wca):
    n = x2.shape[0]
    return pl.pallas_call(
        _inproj_kernel,
        grid=(n // TM_PROJ,),
        in_specs=[
            pl.BlockSpec((TM_PROJ, D_MODEL), lambda i: (i, 0)),
            _resident((1, D_MODEL)),
            _resident(w.shape),
            _resident(wca.shape),
        ],
        out_specs=[
            pl.BlockSpec((TM_PROJ, PROJ_W), lambda i: (i, 0)),
            pl.BlockSpec((TM_PROJ, LANES), lambda i: (i, 0)),
        ],
        out_shape=[
            jax.ShapeDtypeStruct((n, PROJ_W), BF16),
            jax.ShapeDtypeStruct((n, LANES), F32),
        ],
        compiler_params=pltpu.CompilerParams(
            dimension_semantics=("parallel",), vmem_limit_bytes=VMEM_LIMIT),
        name="inproj",
    )(x2, g, w, wca)


def _attn_kernel(lamqk_ref, g_ref, q_ref, k_ref, v_ref, o_ref,
                 qq_scr, vt_scr, s0_scr, s1_scr, c0_scr, c1_scr, m_scr, acc_scr, *, lam_init):
    tq, tk = TQ_ATT, TK_ATT
    qi = pl.program_id(2)

    @pl.when(qi == 0)
    def _():
        extra = jnp.where(lax.broadcasted_iota(jnp.int32, (VT_ROWS - HEAD_DV, tk), 0) == 0, 1.0, 0.0)
        eye = jnp.where(lax.broadcasted_iota(jnp.int32, (HEAD_DV, HEAD_DV), 0)
                        == lax.broadcasted_iota(jnp.int32, (HEAD_DV, HEAD_DV), 1), 1.0, 0.0).astype(BF16)
        for t in range(vt_scr.shape[0]):
            vt_scr[t, 0:HEAD_DV, :] = _dot_nt(eye, v_ref[0, t * tk:(t + 1) * tk, :]).astype(BF16)
            vt_scr[t, HEAD_DV:VT_ROWS, :] = extra.astype(BF16)

    q = q_ref[0].astype(F32) * (DA_HEAD ** -0.5 * math.log2(math.e))
    lane = lax.broadcasted_iota(jnp.int32, (tq, 2 * DA_HEAD), 1)
    qq_scr[...] = jnp.concatenate(
        [jnp.where(lane < DA_HEAD, q, 0.0), jnp.where(lane >= DA_HEAD, q, 0.0)], axis=0).astype(BF16)

    m_scr[...] = jnp.full(m_scr.shape, -jnp.inf, F32)
    acc_scr[...] = jnp.zeros_like(acc_scr)

    def scores(j, s_ref, c_ref, masked):
        start = pl.multiple_of(j * tk, tk)
        s = _dot_nt(k_ref[0, pl.ds(start, tk), :], qq_scr[...])
        if masked:
            key_chunk = (j * tk + lax.broadcasted_iota(jnp.int32, (tk, 1), 0)) // CHUNK
            q_chunk = (qi * tq + lax.broadcasted_iota(jnp.int32, (1, 2 * tq), 1) % tq) // CHUNK
            s = jnp.where(key_chunk <= q_chunk, s, -jnp.inf)
        s_ref[...] = s
        c_ref[...] = jnp.max(s, axis=0, keepdims=True)

    def consume(j, s_ref, c_ref):
        m_prev = m_scr[...]
        m_new = jnp.maximum(m_prev, c_ref[...])
        alpha = jnp.exp2(m_prev - m_new)
        p = jnp.exp2(s_ref[...] - m_new).astype(BF16)
        acc_scr[...] = alpha * acc_scr[...] + _dot(vt_scr[j], p)
        m_scr[...] = m_new

    u = (qi * tq) // tk

    @pl.when(u == 0)
    def _():
        scores(0, s0_scr, c0_scr, True)

    @pl.when(u > 0)
    def _():
        scores(0, s0_scr, c0_scr, False)

    def pair(i, carry):
        scores(2 * i + 1, s1_scr, c1_scr, False)
        consume(2 * i, s0_scr, c0_scr)
        scores(2 * i + 2, s0_scr, c0_scr, False)
        consume(2 * i + 1, s1_scr, c1_scr)
        return carry

    n_pairs = jnp.where(u % 2 == 0, jnp.maximum(u // 2 - 1, 0), u // 2)
    lax.fori_loop(0, n_pairs, pair, 0)

    @pl.when(jnp.logical_and(u % 2 == 0, u > 0))
    def _():
        scores(u - 1, s1_scr, c1_scr, False)
        consume(u - 2, s0_scr, c0_scr)
        scores(u, s0_scr, c0_scr, True)
        consume(u - 1, s1_scr, c1_scr)

    @pl.when(u % 2 == 0)
    def _():
        consume(u, s0_scr, c0_scr)

    @pl.when(u % 2 == 1)
    def _():
        scores(u, s1_scr, c1_scr, True)
        consume(u - 1, s0_scr, c0_scr)
        consume(u, s1_scr, c1_scr)

    lq = lamqk_ref[...]
    lam = (jnp.exp(jnp.sum(lq[0:1] * lq[1:2], axis=-1, keepdims=True))
           - jnp.exp(jnp.sum(lq[2:3] * lq[3:4], axis=-1, keepdims=True)) + lam_init)
    acc = acc_scr[...]
    a = acc[0:HEAD_DV] / acc[HEAD_DV:HEAD_DV + 1]
    out = (a[:, :tq] - lam * a[:, tq:]).T
    o_ref[0] = (_rms(out, g_ref[...]) * (1.0 - lam_init)).astype(BF16)


def _attn(proj3, lamqk, g, lam_init):
    b, s, _ = proj3.shape
    qb, kb, vb = OFF_A // LANES, (OFF_A + QK_A) // LANES, (OFF_A + 2 * QK_A) // LANES
    return pl.pallas_call(
        functools.partial(_attn_kernel, lam_init=lam_init),
        grid=(b, N_HEADS, s // TQ_ATT),
        in_specs=[
            pl.BlockSpec((4, DA_HEAD), lambda bi, h, qi: (0, 0)),
            pl.BlockSpec((1, HEAD_DV), lambda bi, h, qi: (0, 0)),
            pl.BlockSpec((1, TQ_ATT, LANES), lambda bi, h, qi: (bi, qi, qb + h)),
            pl.BlockSpec((1, s, LANES), lambda bi, h, qi: (bi, 0, kb + h)),
            pl.BlockSpec((1, s, LANES), lambda bi, h, qi: (bi, 0, vb + h)),
        ],
        out_specs=pl.BlockSpec((1, TQ_ATT, HEAD_DV), lambda bi, h, qi: (bi, qi, h)),
        out_shape=jax.ShapeDtypeStruct((b, s, V_W), BF16),
        scratch_shapes=[
            pltpu.VMEM((2 * TQ_ATT, 2 * DA_HEAD), BF16),
            pltpu.VMEM((s // TK_ATT, VT_ROWS, TK_ATT), BF16),
            pltpu.VMEM((TK_ATT, 2 * TQ_ATT), F32),
            pltpu.VMEM((TK_ATT, 2 * TQ_ATT), F32),
            pltpu.VMEM((1, 2 * TQ_ATT), F32),
            pltpu.VMEM((1, 2 * TQ_ATT), F32),
            pltpu.VMEM((1, 2 * TQ_ATT), F32),
            pltpu.VMEM((VT_ROWS, 2 * TQ_ATT), F32),
        ],
        compiler_params=pltpu.CompilerParams(
            dimension_semantics=("parallel", "parallel", "arbitrary"), vmem_limit_bytes=VMEM_LIMIT),
        name="diff_attn",
    )(lamqk, g, proj3, proj3, proj3)


def _stack_heads(t):
    lane = lax.broadcasted_iota(jnp.int32, t.shape, 1)
    return jnp.concatenate(
        [jnp.where((lane // LIN_DK) % N_HEADS == h, t, 0.0) for h in range(N_HEADS)],
        axis=0).astype(BF16)


def _head_rms(o, g):
    return jnp.concatenate(
        [_rms(o[:, HEAD_DV * h:HEAD_DV * (h + 1)], g) for h in range(N_HEADS)], axis=1)


def _swap_halves(t):
    lane = lax.broadcasted_iota(jnp.int32, t.shape, 1)
    half = LIN_DK // 2
    return jnp.where((lane % LIN_DK) < half,
                     pltpu.roll(t, QK_L - half, 1), pltpu.roll(t, half, 1))


def _lin_tile(q, k, la, v_ref, gate_ref, gn_ref, o_ref, st_scr):
    c = CHUNK
    n_chunks = T_LIN // c
    sl = [slice(i * c, (i + 1) * c) for i in range(n_chunks)]
    r = lax.broadcasted_iota(jnp.int32, (c, c), 0)
    cc = lax.broadcasted_iota(jnp.int32, (c, c), 1)
    tri = jnp.where(cc <= r, 1.0, 0.0).astype(BF16)
    cum = []
    for s in sl:
        la_hi = la[s].astype(BF16)
        la_lo = (la[s] - la_hi.astype(F32)).astype(BF16)
        cum.append(_dot(tri, la_hi) + _dot(tri, la_lo))

    row_blk = lax.broadcasted_iota(jnp.int32, (c, QK_L), 0) // SUB
    qb, kb = [], []
    for i, s in enumerate(sl):
        q_parts, k_parts = [], []
        for j in range(c // SUB):
            ref = cum[i][SUB * (j + 1) - 1:SUB * (j + 1)]
            q_parts.append(jnp.where(row_blk >= j, q[s] * jnp.exp(cum[i] - ref), 0.0))
            k_parts.append(jnp.where(row_blk == j, k[s] * jnp.exp(ref - cum[i]), 0.0))
        qb.append(jnp.concatenate(q_parts, axis=1).astype(BF16))
        kb.append(_stack_heads(jnp.concatenate(k_parts, axis=1)))

    srow = lax.broadcasted_iota(jnp.int32, (c, N_HEADS * c), 0)
    scol = lax.broadcasted_iota(jnp.int32, (c, N_HEADS * c), 1)
    causal = scol % c <= srow
    sc = [jnp.where(causal, _dot_nt(qb[i], kb[i]), 0.0).astype(BF16) for i in range(n_chunks)]

    vlane = lax.broadcasted_iota(jnp.int32, (1, V_W), 1) // HEAD_DV
    vmask = [jnp.where(vlane == h, 1.0, 0.0).astype(BF16) for h in range(N_HEADS)]
    v = [v_ref[0, s, :] for s in sl]
    inner = [_dot(sc[i], jnp.concatenate([v[i] * vmask[h] for h in range(N_HEADS)], axis=0))
             for i in range(n_chunks)]

    last = [cu[c - 1:c] for cu in cum]
    upd = []
    for i, s in enumerate(sl):
        ke = _stack_heads(k[s] * jnp.exp(last[i] - cum[i]))
        vs = jnp.concatenate(
            [v[i][:, HEAD_DV * h:HEAD_DV * (h + 1)] for h in range(N_HEADS)], axis=0)
        upd.append(_dot_tn(vs, ke))

    st = st_scr[...]
    slane = lax.broadcasted_iota(jnp.int32, st.shape, 1) // LIN_DK
    outs = []
    for i, s in enumerate(sl):
        st_bd = jnp.concatenate(
            [jnp.where(slane == h, st, 0.0) for h in range(N_HEADS)], axis=0).astype(BF16)
        cross = _dot_nt((q[s] * jnp.exp(cum[i])).astype(BF16), st_bd)
        outs.append(inner[i] + cross)
        st = st * jnp.exp(last[i]) + upd[i]
    st_scr[...] = st
    o = jnp.concatenate(outs, axis=0)
    y = jax.nn.silu(gate_ref[0].astype(F32)) * _head_rms(o, gn_ref[...])
    o_ref[0] = y.astype(BF16)


def _ret_kernel(q_ref, k_ref, v_ref, gate_ref, cos_ref, sin_ref, lg_ref, gn_ref, o_ref, st_scr):
    @pl.when(pl.program_id(1) == 0)
    def _():
        st_scr[...] = jnp.zeros_like(st_scr)

    cos = cos_ref[...]
    sin = sin_ref[...]
    q = q_ref[0].astype(F32)
    k = k_ref[0].astype(F32)
    q = q * cos + _swap_halves(q) * sin
    k = (k * cos + _swap_halves(k) * sin) * (LIN_DK ** -0.5)
    la = jnp.broadcast_to(lg_ref[...], (T_LIN, QK_L))
    _lin_tile(q, k, la, v_ref, gate_ref, gn_ref, o_ref, st_scr)


def _gla_kernel(q_ref, k_ref, v_ref, gate_ref, ca_ref, a2_ref, ab_ref, gn_ref, o_ref, st_scr):
    @pl.when(pl.program_id(1) == 0)
    def _():
        st_scr[...] = jnp.zeros_like(st_scr)

    z = _dot(ca_ref[0].astype(BF16), a2_ref[...]) + ab_ref[...]
    la = (jnp.minimum(z, 0.0) - jnp.log(1.0 + jnp.exp(-jnp.abs(z)))) * (1.0 / GLA_TAU)
    q = q_ref[0].astype(F32) * (LIN_DK ** -0.5)
    k = k_ref[0].astype(F32)
    _lin_tile(q, k, la, v_ref, gate_ref, gn_ref, o_ref, st_scr)


def _lin_specs(off):
    qb, kb = off // QK_L, (off + QK_L) // QK_L
    vb, gb = (off + 2 * QK_L) // V_W, (off + 2 * QK_L + V_W) // V_W
    return [
        pl.BlockSpec((1, T_LIN, QK_L), lambda bi, t: (bi, t, qb)),
        pl.BlockSpec((1, T_LIN, QK_L), lambda bi, t: (bi, t, kb)),
        pl.BlockSpec((1, T_LIN, V_W), lambda bi, t: (bi, t, vb)),
        pl.BlockSpec((1, T_LIN, V_W), lambda bi, t: (bi, t, gb)),
    ]


def _lin_call(kern, name, proj3, extra_specs, extra_args, off):
    b, s, _ = proj3.shape
    return pl.pallas_call(
        kern,
        grid=(b, s // T_LIN),
        in_specs=_lin_specs(off) + extra_specs,
        out_specs=pl.BlockSpec((1, T_LIN, V_W), lambda bi, t: (bi, t, 0)),
        out_shape=jax.ShapeDtypeStruct((b, s, V_W), BF16),
        scratch_shapes=[pltpu.VMEM((HEAD_DV, QK_L), F32)],
        compiler_params=pltpu.CompilerParams(
            dimension_semantics=("parallel", "arbitrary"), vmem_limit_bytes=VMEM_LIMIT),
        name=name,
    )(proj3, proj3, proj3, proj3, *extra_args)


def _retention(proj3, cos, sin, lg, gn):
    extra = [
        pl.BlockSpec((T_LIN, QK_L), lambda bi, t: (t, 0)),
        pl.BlockSpec((T_LIN, QK_L), lambda bi, t: (t, 0)),
        pl.BlockSpec((1, QK_L), lambda bi, t: (0, 0)),
        pl.BlockSpec((1, HEAD_DV), lambda bi, t: (0, 0)),
    ]
    return _lin_call(_ret_kernel, "retention", proj3, extra, (cos, sin, lg, gn), OFF_B)


def _gla(proj3, ca3, a2, ab, gn):
    extra = [
        pl.BlockSpec((1, T_LIN, LANES), lambda bi, t: (bi, t, 0)),
        pl.BlockSpec((LANES, QK_L), lambda bi, t: (0, 0)),
        pl.BlockSpec((1, QK_L), lambda bi, t: (0, 0)),
        pl.BlockSpec((1, HEAD_DV), lambda bi, t: (0, 0)),
    ]
    return _lin_call(_gla_kernel, "gla", proj3, extra, (ca3, a2, ab, gn), OFF_C)


def _merge_kernel(x_ref, g0_ref, g1_ref, g2_ref, ya_ref, yb_ref, yc_ref,
                  wa_ref, wb_ref, wc_ref, wo_ref, o_ref):
    def sig(ref):
        return jax.nn.sigmoid(ref[...].astype(F32))

    merged = (sig(g0_ref) * _dot(ya_ref[...], wa_ref[...])
              + sig(g1_ref) * _dot(yb_ref[...], wb_ref[...])
              + sig(g2_ref) * _dot(yc_ref[...], wc_ref[...]))
    o_ref[...] = x_ref[...] + _dot(merged.astype(BF16), wo_ref[...])


def _merge(x2, proj2, ya, yb, yc, wa, wb, wc, wo):
    n = x2.shape[0]
    row = lambda w: pl.BlockSpec((TM_MERGE, w), lambda i: (i, 0))
    full = lambda a: pl.BlockSpec(a.shape, lambda i: (0, 0))
    return pl.pallas_call(
        _merge_kernel,
        grid=(n // TM_MERGE,),
        in_specs=[
            row(D_MODEL),
            pl.BlockSpec((TM_MERGE, D_MODEL), lambda i: (i, 0)),
            pl.BlockSpec((TM_MERGE, D_MODEL), lambda i: (i, 1)),
            pl.BlockSpec((TM_MERGE, D_MODEL), lambda i: (i, 2)),
            row(V_W), row(V_W), row(V_W),
            full(wa), full(wb), full(wc), full(wo),
        ],
        out_specs=row(D_MODEL),
        out_shape=jax.ShapeDtypeStruct((n, D_MODEL), F32),
        compiler_params=pltpu.CompilerParams(
            dimension_semantics=("parallel",), vmem_limit_bytes=VMEM_LIMIT),
        name="merge",
    )(x2, proj2, proj2, proj2, ya, yb, yc, wa, wb, wc, wo)


def _rope_tables(s):
    half = LIN_DK // 2
    inv = ROPE_BASE ** (-jnp.arange(0, LIN_DK, 2, dtype=F32) / LIN_DK)
    ang = jnp.arange(s, dtype=F32)[:, None] * inv[None, :]
    cos, sin = jnp.cos(ang), jnp.sin(ang)
    del half
    cos_t = jnp.tile(jnp.concatenate([cos, cos], axis=-1), (1, N_HEADS))
    sin_t = jnp.tile(jnp.concatenate([-sin, sin], axis=-1), (1, N_HEADS))
    return cos_t, sin_t


def kernel(x, ffn1_norm, ffn1_w1, ffn1_w3, ffn1_w2, mix_norm, w_in, lam_qk, da_norm, ret_norm,
           gla_a2, gla_a_bias, gla_norm, w_branch_a, w_branch_b, w_branch_c, w_out,
           ffn2_norm, ffn2_w1, ffn2_w3, ffn2_w2, final_norm):
    b, s, d = x.shape
    assert d == D_MODEL and s % T_LIN == 0 and s % TK_ATT == 0 and TK_ATT % TQ_ATT == 0
    assert (b * s) % TM_FFN == 0
    pad_f = D_FF_PAD - D_FF

    def prep_ffn(w1, w3, w2):
        return (jnp.pad(w1, ((0, 0), (0, 0), (0, pad_f))).astype(BF16),
                jnp.pad(w3, ((0, 0), (0, 0), (0, pad_f))).astype(BF16),
                jnp.pad(w2, ((0, 0), (0, pad_f), (0, 0))).astype(BF16))

    f1 = prep_ffn(ffn1_w1, ffn1_w3, ffn1_w2)
    f2 = prep_ffn(ffn2_w1, ffn2_w3, ffn2_w2)

    n_abc = OFF_C + 2 * QK_L + 2 * V_W - GATE_W
    w_main = jnp.concatenate(
        [w_in[:, :, n_abc + GLA_RANK:], w_in[:, :, :n_abc]], axis=-1).astype(BF16)
    w_ca = jnp.pad(w_in[:, :, n_abc:n_abc + GLA_RANK],
                   ((0, 0), (0, 0), (0, LANES - GLA_RANK))).astype(BF16)
    a2 = jnp.pad(gla_a2, ((0, 0), (0, LANES - GLA_RANK), (0, 0))).astype(BF16)
    wa, wb, wc, wo = (w.astype(BF16) for w in (w_branch_a, w_branch_b, w_branch_c, w_out))

    cos_t, sin_t = _rope_tables(s)
    log_g = jnp.log1p(-jnp.exp2(-5.0 - jnp.arange(N_HEADS, dtype=F32)))
    lg = jnp.repeat(log_g, LIN_DK)[None, :]
    fin = final_norm[None, :]

    x2 = x.reshape(b * s, d)
    for l in range(DEPTH):
        lam_init = 0.8 - 0.6 * math.exp(-0.3 * l)
        x2 = _ffn(x2, ffn1_norm[l][None, :], f1[0][l], f1[1][l], f1[2][l], fin, False)
        proj2, ca2 = _inproj(x2, mix_norm[l][None, :], w_main[l], w_ca[l])
        proj3 = proj2.reshape(b, s, PROJ_W)
        ya = _attn(proj3, lam_qk[l], da_norm[l][None, :], lam_init)
        yb = _retention(proj3, cos_t, sin_t, lg, ret_norm[l][None, :])
        yc = _gla(proj3, ca2.reshape(b, s, LANES), a2[l], gla_a_bias[l][None, :], gla_norm[l][None, :])
        x2 = _merge(x2, proj2, ya.reshape(b * s, V_W), yb.reshape(b * s, V_W), yc.reshape(b * s, V_W),
                    wa[l], wb[l], wc[l], wo[l])
        x2 = _ffn(x2, ffn2_norm[l][None, :], f2[0][l], f2[1][l], f2[2][l], fin, l == DEPTH - 1)
    return x2.reshape(b, s, d)
```

```python
import functools
import math

import jax
import jax.numpy as jnp
from jax import lax
from jax.experimental import pallas as pl
from jax.experimental.pallas import tpu as pltpu

F32 = jnp.float32
BF16 = jnp.bfloat16

D_MODEL = 1024
N_HEADS = 4
DEPTH = 4
CHUNK = 64
DA_HEAD = 64
HEAD_DV = 128
LIN_DK = 64
GLA_RANK = 16
GLA_TAU = 16.0
ROPE_BASE = 10000.0
D_FF = 2752
EPS = 1e-6

LANES = 128
MXU_DIM = 256

D_FF_PAD = ((D_FF + MXU_DIM - 1) // MXU_DIM) * MXU_DIM
QK_A = N_HEADS * 2 * DA_HEAD
V_W = N_HEADS * HEAD_DV
QK_L = N_HEADS * LIN_DK
GATE_W = 3 * D_MODEL
OFF_A = GATE_W
OFF_B = OFF_A + 2 * QK_A + V_W
OFF_C = OFF_B + 2 * QK_L + 2 * V_W
PROJ_W = OFF_C + 2 * QK_L + 2 * V_W

TM_FFN = 512
TF_FFN = MXU_DIM
NF_FFN = D_FF_PAD // TF_FFN
TM_PROJ = 512
TN_PROJ = 512
TQ_ATT = 512
TK_ATT = 512
VT_ROWS = HEAD_DV + 16
T_LIN = 1024
SUB = 16
TM_MERGE = 512
VMEM_LIMIT = 48 * 1024 * 1024


def _rms(xf, g):
    return xf * lax.rsqrt(jnp.mean(xf * xf, axis=-1, keepdims=True) + EPS) * g


def _dot(a, b):
    return jnp.dot(a, b, preferred_element_type=F32)


def _dot_nt(a, b):
    return lax.dot_general(a, b, (((1,), (1,)), ((), ())), preferred_element_type=F32)


def _dot_tn(a, b):
    return lax.dot_general(a, b, (((0,), (0,)), ((), ())), preferred_element_type=F32)


def _ffn_kernel(x_ref, g_ref, w1_ref, w3_ref, w2_ref, gf_ref, o_ref, *, final):
    x = x_ref[...]
    h = _rms(x, g_ref[...]).astype(BF16)
    acc = None
    for c in range(NF_FFN):
        cs = slice(c * TF_FFN, (c + 1) * TF_FFN)
        a = _dot(h, w1_ref[:, cs])
        b = _dot(h, w3_ref[:, cs])
        t = (a * jax.nn.sigmoid(a) * b).astype(BF16)
        d = _dot(t, w2_ref[cs, :])
        acc = d if acc is None else acc + d
    y = x + 0.5 * acc
    if final:
        y = _rms(y, gf_ref[...])
    o_ref[...] = y


def _resident(shape):
    return pl.BlockSpec(shape, lambda i: (0,) * len(shape), pipeline_mode=pl.Buffered(1))


def _ffn(x2, g, w1, w3, w2, gf, final):
    n = x2.shape[0]
    return pl.pallas_call(
        functools.partial(_ffn_kernel, final=final),
        grid=(n // TM_FFN,),
        in_specs=[
            pl.BlockSpec((TM_FFN, D_MODEL), lambda i: (i, 0)),
            _resident((1, D_MODEL)),
            _resident(w1.shape), _resident(w3.shape), _resident(w2.shape),
            _resident((1, D_MODEL)),
        ],
        out_specs=pl.BlockSpec((TM_FFN, D_MODEL), lambda i: (i, 0)),
        out_shape=jax.ShapeDtypeStruct((n, D_MODEL), F32),
        compiler_params=pltpu.CompilerParams(
            dimension_semantics=("parallel",), vmem_limit_bytes=VMEM_LIMIT),
        name="ffn",
    )(x2, g, w1, w3, w2, gf)


def _inproj_kernel(x_ref, g_ref, w_ref, wca_ref, proj_ref, ca_ref):
    h = _rms(x_ref[...], g_ref[...]).astype(BF16)
    ca_ref[...] = _dot(h, wca_ref[...])
    for c in range(PROJ_W // TN_PROJ):
        cs = slice(c * TN_PROJ, (c + 1) * TN_PROJ)
        proj_ref[:, cs] = _dot(h, w_ref[:, cs]).astype(BF16)


def _inproj(x2, g, w, wca):
    n = x2.shape[0]
    return pl.pallas_call(
        _inproj_kernel,
        grid=(n // TM_PROJ,),
        in_specs=[
            pl.BlockSpec((TM_PROJ, D_MODEL), lambda i: (i, 0)),
            _resident((1, D_MODEL)),
            _resident(w.shape),
            _resident(wca.shape),
        ],
        out_specs=[
            pl.BlockSpec((TM_PROJ, PROJ_W), lambda i: (i, 0)),
            pl.BlockSpec((TM_PROJ, LANES), lambda i: (i, 0)),
        ],
        out_shape=[
            jax.ShapeDtypeStruct((n, PROJ_W), BF16),
            jax.ShapeDtypeStruct((n, LANES), F32),
        ],
        compiler_params=pltpu.CompilerParams(
            dimension_semantics=("parallel",), vmem_limit_bytes=VMEM_LIMIT),
        name="inproj",
    )(x2, g, w, wca)


def _attn_kernel(lamqk_ref, g_ref, q_ref, k_ref, v_ref, o_ref,
                 qq_scr, vt_scr, s0_scr, s1_scr, c0_scr, c1_scr, m_scr, acc_scr, *, lam_init):
    tq, tk = TQ_ATT, TK_ATT

    extra = jnp.where(lax.broadcasted_iota(jnp.int32, (VT_ROWS - HEAD_DV, tk), 0) == 0, 1.0, 0.0)
    eye = jnp.where(lax.broadcasted_iota(jnp.int32, (HEAD_DV, HEAD_DV), 0)
                    == lax.broadcasted_iota(jnp.int32, (HEAD_DV, HEAD_DV), 1), 1.0, 0.0).astype(BF16)
    for t in range(vt_scr.shape[0]):
        vt_scr[t, 0:HEAD_DV, :] = _dot_nt(eye, v_ref[0, t * tk:(t + 1) * tk, :]).astype(BF16)
        vt_scr[t, HEAD_DV:VT_ROWS, :] = extra.astype(BF16)

    lq = lamqk_ref[...]
    lam = (jnp.exp(jnp.sum(lq[0:1] * lq[1:2], axis=-1, keepdims=True))
           - jnp.exp(jnp.sum(lq[2:3] * lq[3:4], axis=-1, keepdims=True)) + lam_init)

    def q_tile(qi, carry):
        q_rows = pl.ds(pl.multiple_of(qi * tq, tq), tq)
        q = q_ref[0, q_rows, :].astype(F32) * (DA_HEAD ** -0.5 * math.log2(math.e))
        lane = lax.broadcasted_iota(jnp.int32, (tq, 2 * DA_HEAD), 1)
        qq_scr[...] = jnp.concatenate(
            [jnp.where(lane < DA_HEAD, q, 0.0), jnp.where(lane >= DA_HEAD, q, 0.0)], axis=0).astype(BF16)

        m_scr[...] = jnp.full(m_scr.shape, -jnp.inf, F32)
        acc_scr[...] = jnp.zeros_like(acc_scr)

        def scores_half(j, s_ref, c_ref, masked, hf):
            start = pl.multiple_of(j * tk, tk)
            cols = slice(hf * tq, (hf + 1) * tq)
            s = _dot_nt(k_ref[0, pl.ds(start, tk), :], qq_scr[cols, :])
            if masked:
                key_chunk = (j * tk + lax.broadcasted_iota(jnp.int32, (tk, 1), 0)) // CHUNK
                q_chunk = (qi * tq + lax.broadcasted_iota(jnp.int32, (1, tq), 1)) // CHUNK
                s = jnp.where(key_chunk <= q_chunk, s, -jnp.inf)
            s_ref[:, cols] = s
            c_ref[:, cols] = jnp.max(s, axis=0, keepdims=True)

        def consume_half(j, s_ref, c_ref, hf):
            cols = slice(hf * tq, (hf + 1) * tq)
            m_prev = m_scr[:, cols]
            m_new = jnp.maximum(m_prev, c_ref[:, cols])
            alpha = jnp.exp2(m_prev - m_new)
            p = jnp.exp2(s_ref[:, cols] - m_new).astype(BF16)
            acc_scr[:, cols] = alpha * acc_scr[:, cols] + _dot(vt_scr[j], p)
            m_scr[:, cols] = m_new

        def scores(j, s_ref, c_ref, masked):
            for hf in (0, 1):
                scores_half(j, s_ref, c_ref, masked, hf)

        def consume(j, s_ref, c_ref):
            for hf in (0, 1):
                consume_half(j, s_ref, c_ref, hf)

        def overlapped(j_next, s_next, c_next, masked, j_cur, s_cur, c_cur):
            for hf in (0, 1):
                scores_half(j_next, s_next, c_next, masked, hf)
                consume_half(j_cur, s_cur, c_cur, hf)

        u = (qi * tq) // tk

        @pl.when(u == 0)
        def _():
            scores(0, s0_scr, c0_scr, True)

        @pl.when(u > 0)
        def _():
            scores(0, s0_scr, c0_scr, False)

        def pair(i, c):
            overlapped(2 * i + 1, s1_scr, c1_scr, False, 2 * i, s0_scr, c0_scr)
            overlapped(2 * i + 2, s0_scr, c0_scr, False, 2 * i + 1, s1_scr, c1_scr)
            return c

        n_pairs = jnp.where(u % 2 == 0, jnp.maximum(u // 2 - 1, 0), u // 2)
        lax.fori_loop(0, n_pairs, pair, 0)

        @pl.when(jnp.logical_and(u % 2 == 0, u > 0))
        def _():
            overlapped(u - 1, s1_scr, c1_scr, False, u - 2, s0_scr, c0_scr)
            overlapped(u, s0_scr, c0_scr, True, u - 1, s1_scr, c1_scr)

        @pl.when(u % 2 == 0)
        def _():
            consume(u, s0_scr, c0_scr)

        @pl.when(u % 2 == 1)
        def _():
            overlapped(u, s1_scr, c1_scr, True, u - 1, s0_scr, c0_scr)
            consume(u, s1_scr, c1_scr)

        acc = acc_scr[...]
        a = acc[0:HEAD_DV] / acc[HEAD_DV:HEAD_DV + 1]
        out = (a[:, :tq] - lam * a[:, tq:]).T
        o_ref[0, q_rows, :] = (_rms(out, g_ref[...]) * (1.0 - lam_init)).astype(BF16)
        return carry

    lax.fori_loop(0, q_ref.shape[1] // tq, q_tile, 0)


def _attn(proj3, lamqk, g, lam_init):
    b, s, _ = proj3.shape
    qb, kb, vb = OFF_A // LANES, (OFF_A + QK_A) // LANES, (OFF_A + 2 * QK_A) // LANES
    seq = lambda blk: pl.BlockSpec((1, s, LANES), lambda bi, h: (bi, 0, blk + h))
    return pl.pallas_call(
        functools.partial(_attn_kernel, lam_init=lam_init),
        grid=(b, N_HEADS),
        in_specs=[
            pl.BlockSpec((4, DA_HEAD), lambda bi, h: (0, 0)),
            pl.BlockSpec((1, HEAD_DV), lambda bi, h: (0, 0)),
            seq(qb), seq(kb), seq(vb),
        ],
        out_specs=seq(0),
        out_shape=jax.ShapeDtypeStruct((b, s, V_W), BF16),
        scratch_shapes=[
            pltpu.VMEM((2 * TQ_ATT, 2 * DA_HEAD), BF16),
            pltpu.VMEM((s // TK_ATT, VT_ROWS, TK_ATT), BF16),
            pltpu.VMEM((TK_ATT, 2 * TQ_ATT), F32),
            pltpu.VMEM((TK_ATT, 2 * TQ_ATT), F32),
            pltpu.VMEM((1, 2 * TQ_ATT), F32),
            pltpu.VMEM((1, 2 * TQ_ATT), F32),
            pltpu.VMEM((1, 2 * TQ_ATT), F32),
            pltpu.VMEM((VT_ROWS, 2 * TQ_ATT), F32),
        ],
        compiler_params=pltpu.CompilerParams(
            dimension_semantics=("parallel", "parallel"), vmem_limit_bytes=VMEM_LIMIT),
        name="diff_attn",
    )(lamqk, g, proj3, proj3, proj3)


def _stack_heads(t):
    lane = lax.broadcasted_iota(jnp.int32, t.shape, 1)
    return jnp.concatenate(
        [jnp.where((lane // LIN_DK) % N_HEADS == h, t, 0.0) for h in range(N_HEADS)],
        axis=0).astype(BF16)


def _head_rms(o, g):
    return jnp.concatenate(
        [_rms(o[:, HEAD_DV * h:HEAD_DV * (h + 1)], g) for h in range(N_HEADS)], axis=1)


def _swap_halves(t):
    lane = lax.broadcasted_iota(jnp.int32, t.shape, 1)
    half = LIN_DK // 2
    return jnp.where((lane % LIN_DK) < half,
                     pltpu.roll(t, QK_L - half, 1), pltpu.roll(t, half, 1))


def _lin_tile(q, k, la, v_ref, gate_ref, gn_ref, o_ref, st_scr):
    c = CHUNK
    n_chunks = T_LIN // c
    sl = [slice(i * c, (i + 1) * c) for i in range(n_chunks)]
    r = lax.broadcasted_iota(jnp.int32, (c, c), 0)
    cc = lax.broadcasted_iota(jnp.int32, (c, c), 1)
    tri = jnp.where(cc <= r, 1.0, 0.0).astype(BF16)
    cum = []
    for s in sl:
        la_hi = la[s].astype(BF16)
        la_lo = (la[s] - la_hi.astype(F32)).astype(BF16)
        cum.append(_dot(tri, la_hi) + _dot(tri, la_lo))

    row_blk = lax.broadcasted_iota(jnp.int32, (c, QK_L), 0) // SUB
    qb, kb = [], []
    for i, s in enumerate(sl):
        q_parts, k_parts = [], []
        for j in range(c // SUB):
            ref = cum[i][SUB * (j + 1) - 1:SUB * (j + 1)]
            q_parts.append(jnp.where(row_blk >= j, q[s] * jnp.exp(cum[i] - ref), 0.0))
            k_parts.append(jnp.where(row_blk == j, k[s] * jnp.exp(ref - cum[i]), 0.0))
        qb.append(jnp.concatenate(q_parts, axis=1).astype(BF16))
        kb.append(_stack_heads(jnp.concatenate(k_parts, axis=1)))

    srow = lax.broadcasted_iota(jnp.int32, (c, N_HEADS * c), 0)
    scol = lax.broadcasted_iota(jnp.int32, (c, N_HEADS * c), 1)
    causal = scol % c <= srow
    sc = [jnp.where(causal, _dot_nt(qb[i], kb[i]), 0.0).astype(BF16) for i in range(n_chunks)]

    vlane = lax.broadcasted_iota(jnp.int32, (1, V_W), 1) // HEAD_DV
    vmask = [jnp.where(vlane == h, 1.0, 0.0).astype(BF16) for h in range(N_HEADS)]
    v = [v_ref[0, s, :] for s in sl]
    inner = [_dot(sc[i], jnp.concatenate([v[i] * vmask[h] for h in range(N_HEADS)], axis=0))
             for i in range(n_chunks)]

    last = [cu[c - 1:c] for cu in cum]
    upd = []
    for i, s in enumerate(sl):
        ke = _stack_heads(k[s] * jnp.exp(last[i] - cum[i]))
        vs = jnp.concatenate(
            [v[i][:, HEAD_DV * h:HEAD_DV * (h + 1)] for h in range(N_HEADS)], axis=0)
        upd.append(_dot_tn(vs, ke))

    st = st_scr[...]
    slane = lax.broadcasted_iota(jnp.int32, st.shape, 1) // LIN_DK
    outs = []
    for i, s in enumerate(sl):
        st_bd = jnp.concatenate(
            [jnp.where(slane == h, st, 0.0) for h in range(N_HEADS)], axis=0).astype(BF16)
        cross = _dot_nt((q[s] * jnp.exp(cum[i])).astype(BF16), st_bd)
        outs.append(inner[i] + cross)
        st = st * jnp.exp(last[i]) + upd[i]
    st_scr[...] = st
    o = jnp.concatenate(outs, axis=0)
    y = jax.nn.silu(gate_ref[0].astype(F32)) * _head_rms(o, gn_ref[...])
    o_ref[0] = y.astype(BF16)


def _ret_kernel(q_ref, k_ref, v_ref, gate_ref, cos_ref, sin_ref, lg_ref, gn_ref, o_ref, st_scr):
    @pl.when(pl.program_id(1) == 0)
    def _():
        st_scr[...] = jnp.zeros_like(st_scr)

    cos = cos_ref[...]
    sin = sin_ref[...]
    q = q_ref[0].astype(F32)
    k = k_ref[0].astype(F32)
    q = q * cos + _swap_halves(q) * sin
    k = (k * cos + _swap_halves(k) * sin) * (LIN_DK ** -0.5)
    la = jnp.broadcast_to(lg_ref[...], (T_LIN, QK_L))
    _lin_tile(q, k, la, v_ref, gate_ref, gn_ref, o_ref, st_scr)


def _gla_kernel(q_ref, k_ref, v_ref, gate_ref, ca_ref, a2_ref, ab_ref, gn_ref, o_ref, st_scr):
    @pl.when(pl.program_id(1) == 0)
    def _():
        st_scr[...] = jnp.zeros_like(st_scr)

    z = _dot(ca_ref[0].astype(BF16), a2_ref[...]) + ab_ref[...]
    la = (jnp.minimum(z, 0.0) - jnp.log(1.0 + jnp.exp(-jnp.abs(z)))) * (1.0 / GLA_TAU)
    q = q_ref[0].astype(F32) * (LIN_DK ** -0.5)
    k = k_ref[0].astype(F32)
    _lin_tile(q, k, la, v_ref, gate_ref, gn_ref, o_ref, st_scr)


def _lin_specs(off):
    qb, kb = off // QK_L, (off + QK_L) // QK_L
    vb, gb = (off + 2 * QK_L) // V_W, (off + 2 * QK_L + V_W) // V_W
    return [
        pl.BlockSpec((1, T_LIN, QK_L), lambda bi, t: (bi, t, qb)),
        pl.BlockSpec((1, T_LIN, QK_L), lambda bi, t: (bi, t, kb)),
        pl.BlockSpec((1, T_LIN, V_W), lambda bi, t: (bi, t, vb)),
        pl.BlockSpec((1, T_LIN, V_W), lambda bi, t: (bi, t, gb)),
    ]


def _lin_call(kern, name, proj3, extra_specs, extra_args, off):
    b, s, _ = proj3.shape
    return pl.pallas_call(
        kern,
        grid=(b, s // T_LIN),
        in_specs=_lin_specs(off) + extra_specs,
        out_specs=pl.BlockSpec((1, T_LIN, V_W), lambda bi, t: (bi, t, 0)),
        out_shape=jax.ShapeDtypeStruct((b, s, V_W), BF16),
        scratch_shapes=[pltpu.VMEM((HEAD_DV, QK_L), F32)],
        compiler_params=pltpu.CompilerParams(
            dimension_semantics=("parallel", "arbitrary"), vmem_limit_bytes=VMEM_LIMIT),
        name=name,
    )(proj3, proj3, proj3, proj3, *extra_args)


def _retention(proj3, cos, sin, lg, gn):
    extra = [
        pl.BlockSpec((T_LIN, QK_L), lambda bi, t: (t, 0)),
        pl.BlockSpec((T_LIN, QK_L), lambda bi, t: (t, 0)),
        pl.BlockSpec((1, QK_L), lambda bi, t: (0, 0)),
        pl.BlockSpec((1, HEAD_DV), lambda bi, t: (0, 0)),
    ]
    return _lin_call(_ret_kernel, "retention", proj3, extra, (cos, sin, lg, gn), OFF_B)


def _gla(proj3, ca3, a2, ab, gn):
    extra = [
        pl.BlockSpec((1, T_LIN, LANES), lambda bi, t: (bi, t, 0)),
        pl.BlockSpec((LANES, QK_L), lambda bi, t: (0, 0)),
        pl.BlockSpec((1, QK_L), lambda bi, t: (0, 0)),
        pl.BlockSpec((1, HEAD_DV), lambda bi, t: (0, 0)),
    ]
    return _lin_call(_gla_kernel, "gla", proj3, extra, (ca3, a2, ab, gn), OFF_C)


def _merge_kernel(x_ref, g0_ref, g1_ref, g2_ref, ya_ref, yb_ref, yc_ref,
                  wa_ref, wb_ref, wc_ref, wo_ref, o_ref):
    def sig(ref):
        return jax.nn.sigmoid(ref[...].astype(F32))

    merged = (sig(g0_ref) * _dot(ya_ref[...], wa_ref[...])
              + sig(g1_ref) * _dot(yb_ref[...], wb_ref[...])
              + sig(g2_ref) * _dot(yc_ref[...], wc_ref[...]))
    o_ref[...] = x_ref[...] + _dot(merged.astype(BF16), wo_ref[...])


def _merge(x2, proj2, ya, yb, yc, wa, wb, wc, wo):
    n = x2.shape[0]
    row = lambda w: pl.BlockSpec((TM_MERGE, w), lambda i: (i, 0))
    full = lambda a: pl.BlockSpec(a.shape, lambda i: (0, 0))
    return pl.pallas_call(
        _merge_kernel,
        grid=(n // TM_MERGE,),
        in_specs=[
            row(D_MODEL),
            pl.BlockSpec((TM_MERGE, D_MODEL), lambda i: (i, 0)),
            pl.BlockSpec((TM_MERGE, D_MODEL), lambda i: (i, 1)),
            pl.BlockSpec((TM_MERGE, D_MODEL), lambda i: (i, 2)),
            row(V_W), row(V_W), row(V_W),
            full(wa), full(wb), full(wc), full(wo),
        ],
        out_specs=row(D_MODEL),
        out_shape=jax.ShapeDtypeStruct((n, D_MODEL), F32),
        compiler_params=pltpu.CompilerParams(
            dimension_semantics=("parallel",), vmem_limit_bytes=VMEM_LIMIT),
        name="merge",
    )(x2, proj2, proj2, proj2, ya, yb, yc, wa, wb, wc, wo)


def _rope_tables(s):
    half = LIN_DK // 2
    inv = ROPE_BASE ** (-jnp.arange(0, LIN_DK, 2, dtype=F32) / LIN_DK)
    ang = jnp.arange(s, dtype=F32)[:, None] * inv[None, :]
    cos, sin = jnp.cos(ang), jnp.sin(ang)
    del half
    cos_t = jnp.tile(jnp.concatenate([cos, cos], axis=-1), (1, N_HEADS))
    sin_t = jnp.tile(jnp.concatenate([-sin, sin], axis=-1), (1, N_HEADS))
    return cos_t, sin_t


def kernel(x, ffn1_norm, ffn1_w1, ffn1_w3, ffn1_w2, mix_norm, w_in, lam_qk, da_norm, ret_norm,
           gla_a2, gla_a_bias, gla_norm, w_branch_a, w_branch_b, w_branch_c, w_out,
           ffn2_norm, ffn2_w1, ffn2_w3, ffn2_w2, final_norm):
    b, s, d = x.shape
    assert d == D_MODEL and s % T_LIN == 0 and s % TK_ATT == 0 and TK_ATT % TQ_ATT == 0
    assert (b * s) % TM_FFN == 0
    pad_f = D_FF_PAD - D_FF

    def prep_ffn(w1, w3, w2):
        return (jnp.pad(w1, ((0, 0), (0, 0), (0, pad_f))).astype(BF16),
                jnp.pad(w3, ((0, 0), (0, 0), (0, pad_f))).astype(BF16),
                jnp.pad(w2, ((0, 0), (0, pad_f), (0, 0))).astype(BF16))

    f1 = prep_ffn(ffn1_w1, ffn1_w3, ffn1_w2)
    f2 = prep_ffn(ffn2_w1, ffn2_w3, ffn2_w2)

    n_abc = OFF_C + 2 * QK_L + 2 * V_W - GATE_W
    w_main = jnp.concatenate(
        [w_in[:, :, n_abc + GLA_RANK:], w_in[:, :, :n_abc]], axis=-1).astype(BF16)
    w_ca = jnp.pad(w_in[:, :, n_abc:n_abc + GLA_RANK],
                   ((0, 0), (0, 0), (0, LANES - GLA_RANK))).astype(BF16)
    a2 = jnp.pad(gla_a2, ((0, 0), (0, LANES - GLA_RANK), (0, 0))).astype(BF16)
    wa, wb, wc, wo = (w.astype(BF16) for w in (w_branch_a, w_branch_b, w_branch_c, w_out))

    cos_t, sin_t = _rope_tables(s)
    log_g = jnp.log1p(-jnp.exp2(-5.0 - jnp.arange(N_HEADS, dtype=F32)))
    lg = jnp.repeat(log_g, LIN_DK)[None, :]
    fin = final_norm[None, :]

    x2 = x.reshape(b * s, d)
    for l in range(DEPTH):
        lam_init = 0.8 - 0.6 * math.exp(-0.3 * l)
        x2 = _ffn(x2, ffn1_norm[l][None, :], f1[0][l], f1[1][l], f1[2][l], fin, False)
        proj2, ca2 = _inproj(x2, mix_norm[l][None, :], w_main[l], w_ca[l])
        proj3 = proj2.reshape(b, s, PROJ_W)
        ya = _attn(proj3, lam_qk[l], da_norm[l][None, :], lam_init)
        yb = _retention(proj3, cos_t, sin_t, lg, ret_norm[l][None, :])
        yc = _gla(proj3, ca2.reshape(b, s, LANES), a2[l], gla_a_bias[l][None, :], gla_norm[l][None, :])
        x2 = _merge(x2, proj2, ya.reshape(b * s, V_W), yb.reshape(b * s, V_W), yc.reshape(b * s, V_W),
                    wa[l], wb[l], wc[l], wo[l])
        x2 = _ffn(x2, ffn2_norm[l][None, :], f2[0][l], f2[1][l], f2[2][l], fin, l == DEPTH - 1)
    return x2.reshape(b, s, d)
```

```python
import functools
import math

import jax
import jax.numpy as jnp
from jax import lax
from jax.experimental import pallas as pl
from jax.experimental.pallas import tpu as pltpu

F32 = jnp.float32
BF16 = jnp.bfloat16

D_MODEL = 1024
N_HEADS = 4
DEPTH = 4
CHUNK = 64
DA_HEAD = 64
HEAD_DV = 128
LIN_DK = 64
GLA_RANK = 16
GLA_TAU = 16.0
ROPE_BASE = 10000.0
D_FF = 2752
EPS = 1e-6

LANES = 128
MXU_DIM = 256

D_FF_PAD = ((D_FF + MXU_DIM - 1) // MXU_DIM) * MXU_DIM
QK_A = N_HEADS * 2 * DA_HEAD
V_W = N_HEADS * HEAD_DV
QK_L = N_HEADS * LIN_DK
GATE_W = 3 * D_MODEL
OFF_A = GATE_W
OFF_B = OFF_A + 2 * QK_A + V_W
OFF_C = OFF_B + 2 * QK_L + 2 * V_W
PROJ_W = OFF_C + 2 * QK_L + 2 * V_W

TM_FFN = 512
TF_FFN = MXU_DIM
NF_FFN = D_FF_PAD // TF_FFN
TM_PROJ = 512
TN_PROJ = 512
TQ_ATT = 512
TK_ATT = 512
VT_ROWS = HEAD_DV + 16
T_LIN = 1024
SUB = 16
TM_MERGE = 512
VMEM_LIMIT = 48 * 1024 * 1024


def _rms(xf, g):
    return xf * lax.rsqrt(jnp.mean(xf * xf, axis=-1, keepdims=True) + EPS) * g


def _dot(a, b):
    return jnp.dot(a, b, preferred_element_type=F32)


def _dot_nt(a, b):
    return lax.dot_general(a, b, (((1,), (1,)), ((), ())), preferred_element_type=F32)


def _dot_tn(a, b):
    return lax.dot_general(a, b, (((0,), (0,)), ((), ())), preferred_element_type=F32)


def _ffn_kernel(x_ref, g_ref, w1_ref, w3_ref, w2_ref, gf_ref, o_ref, *, final):
    x = x_ref[...]
    h = _rms(x, g_ref[...]).astype(BF16)
    acc = None
    for c in range(NF_FFN):
        cs = slice(c * TF_FFN, (c + 1) * TF_FFN)
        a = _dot(h, w1_ref[:, cs])
        b = _dot(h, w3_ref[:, cs])
        t = (a * jax.nn.sigmoid(a) * b).astype(BF16)
        d = _dot(t, w2_ref[cs, :])
        acc = d if acc is None else acc + d
    y = x + 0.5 * acc
    if final:
        y = _rms(y, gf_ref[...])
    o_ref[...] = y


def _resident(shape):
    return pl.BlockSpec(shape, lambda i: (0,) * len(shape), pipeline_mode=pl.Buffered(1))


def _ffn(x2, g, w1, w3, w2, gf, final):
    n = x2.shape[0]
    return pl.pallas_call(
        functools.partial(_ffn_kernel, final=final),
        grid=(n // TM_FFN,),
        in_specs=[
            pl.BlockSpec((TM_FFN, D_MODEL), lambda i: (i, 0)),
            _resident((1, D_MODEL)),
            _resident(w1.shape), _resident(w3.shape), _resident(w2.shape),
            _resident((1, D_MODEL)),
        ],
        out_specs=pl.BlockSpec((TM_FFN, D_MODEL), lambda i: (i, 0)),
        out_shape=jax.ShapeDtypeStruct((n, D_MODEL), F32),
        compiler_params=pltpu.CompilerParams(
            dimension_semantics=("parallel",), vmem_limit_bytes=VMEM_LIMIT),
        name="ffn",
    )(x2, g, w1, w3, w2, gf)


def _inproj_kernel(x_ref, g_ref, w_ref, wca_ref, proj_ref, ca_ref):
    h = _rms(x_ref[...], g_ref[...]).astype(BF16)
    ca_ref[...] = _dot(h, wca_ref[...])
    for c in range(PROJ_W // TN_PROJ):
        cs = slice(c * TN_PROJ, (c + 1) * TN_PROJ)
        proj_ref[:, cs] = _dot(h, w_ref[:, cs]).astype(BF16)


def _inproj(x2, g, w, wca):
    n = x2.shape[0]
    return pl.pallas_call(
        _inproj_kernel,
        grid=(n // TM_PROJ,),
        in_specs=[
            pl.BlockSpec((TM_PROJ, D_MODEL), lambda i: (i, 0)),
            _resident((1, D_MODEL)),
            _resident(w.shape),
            _resident(wca.shape),
        ],
        out_specs=[
            pl.BlockSpec((TM_PROJ, PROJ_W), lambda i: (i, 0)),
            pl.BlockSpec((TM_PROJ, LANES), lambda i: (i, 0)),
        ],
        out_shape=[
            jax.ShapeDtypeStruct((n, PROJ_W), BF16),
            jax.ShapeDtypeStruct((n, LANES), F32),
        ],
        compiler_params=pltpu.CompilerParams(
            dimension_semantics=("parallel",), vmem_limit_bytes=VMEM_LIMIT),
        name="inproj",
    )(x2, g, w, wca)


def _attn_kernel(lamqk_ref, g_ref, q_ref, k_ref, v_ref, o_ref,
                 qq_scr, vt_scr, s0_scr, s1_scr, s2_scr, c0_scr, c1_scr, c2_scr, m_scr, acc_scr,
                 *, lam_init):
    tq, tk = TQ_ATT, TK_ATT

    extra = jnp.where(lax.broadcasted_iota(jnp.int32, (VT_ROWS - HEAD_DV, tk), 0) == 0, 1.0, 0.0)
    eye = jnp.where(lax.broadcasted_iota(jnp.int32, (HEAD_DV, HEAD_DV), 0)
                    == lax.broadcasted_iota(jnp.int32, (HEAD_DV, HEAD_DV), 1), 1.0, 0.0).astype(BF16)
    for t in range(vt_scr.shape[0]):
        vt_scr[t, 0:HEAD_DV, :] = _dot_nt(eye, v_ref[0, t * tk:(t + 1) * tk, :]).astype(BF16)
        vt_scr[t, HEAD_DV:VT_ROWS, :] = extra.astype(BF16)

    lq = lamqk_ref[...]
    lam = (jnp.exp(jnp.sum(lq[0:1] * lq[1:2], axis=-1, keepdims=True))
           - jnp.exp(jnp.sum(lq[2:3] * lq[3:4], axis=-1, keepdims=True)) + lam_init)

    def build_queries(qi):
        q_rows = pl.ds(pl.multiple_of(qi * tq, tq), tq)
        q = q_ref[0, q_rows, :].astype(F32) * (DA_HEAD ** -0.5 * math.log2(math.e))
        lane = lax.broadcasted_iota(jnp.int32, (tq, 2 * DA_HEAD), 1)
        qq_scr[...] = jnp.concatenate(
            [jnp.where(lane < DA_HEAD, q, 0.0), jnp.where(lane >= DA_HEAD, q, 0.0)], axis=0).astype(BF16)

    def scores_half(qi, j, s_ref, c_ref, masked, hf):
        start = pl.multiple_of(j * tk, tk)
        cols = slice(hf * tq, (hf + 1) * tq)
        s = _dot_nt(k_ref[0, pl.ds(start, tk), :], qq_scr[cols, :])
        if masked:
            key_chunk = (j * tk + lax.broadcasted_iota(jnp.int32, (tk, 1), 0)) // CHUNK
            q_chunk = (qi * tq + lax.broadcasted_iota(jnp.int32, (1, tq), 1)) // CHUNK
            s = jnp.where(key_chunk <= q_chunk, s, -jnp.inf)
        s_ref[:, cols] = s
        c_ref[:, cols] = jnp.max(s, axis=0, keepdims=True)

    def consume_half(j, s_ref, c_ref, hf):
        cols = slice(hf * tq, (hf + 1) * tq)
        m_prev = m_scr[:, cols]
        m_new = jnp.maximum(m_prev, c_ref[:, cols])
        alpha = jnp.exp2(m_prev - m_new)
        p = jnp.exp2(s_ref[:, cols] - m_new).astype(BF16)
        acc_scr[:, cols] = alpha * acc_scr[:, cols] + _dot(vt_scr[j], p)
        m_scr[:, cols] = m_new

    def scores(qi, j, s_ref, c_ref, masked):
        for hf in (0, 1):
            scores_half(qi, j, s_ref, c_ref, masked, hf)

    def consume(j, s_ref, c_ref):
        for hf in (0, 1):
            consume_half(j, s_ref, c_ref, hf)

    def overlapped(qi, j_next, s_next, c_next, masked, j_cur, s_cur, c_cur):
        for hf in (0, 1):
            scores_half(qi, j_next, s_next, c_next, masked, hf)
            consume_half(j_cur, s_cur, c_cur, hf)

    n_q = q_ref.shape[1] // tq
    build_queries(0)
    scores(0, 0, s2_scr, c2_scr, True)

    def q_tile(qi, carry):
        m_scr[...] = jnp.full(m_scr.shape, -jnp.inf, F32)
        acc_scr[...] = jnp.zeros_like(acc_scr)
        u = qi

        @pl.when(u == 0)
        def _():
            consume(0, s2_scr, c2_scr)

        @pl.when(u == 1)
        def _():
            overlapped(qi, 1, s1_scr, c1_scr, True, 0, s2_scr, c2_scr)
            consume(1, s1_scr, c1_scr)

        @pl.when(u > 1)
        def _():
            overlapped(qi, 1, s1_scr, c1_scr, False, 0, s2_scr, c2_scr)

        w = u - 1

        def pair(i, c):
            overlapped(qi, 2 * i + 2, s0_scr, c0_scr, False, 2 * i + 1, s1_scr, c1_scr)
            overlapped(qi, 2 * i + 3, s1_scr, c1_scr, False, 2 * i + 2, s0_scr, c0_scr)
            return c

        n_pairs = jnp.where(w % 2 == 0, jnp.maximum(w // 2 - 1, 0), jnp.maximum(w, 0) // 2)
        lax.fori_loop(0, n_pairs, pair, 0)

        @pl.when(jnp.logical_and(w % 2 == 0, w > 0))
        def _():
            overlapped(qi, u - 1, s0_scr, c0_scr, False, u - 2, s1_scr, c1_scr)
            overlapped(qi, u, s1_scr, c1_scr, True, u - 1, s0_scr, c0_scr)
            consume(u, s1_scr, c1_scr)

        @pl.when(jnp.logical_and(w % 2 == 1, w > 0))
        def _():
            overlapped(qi, u, s0_scr, c0_scr, True, u - 1, s1_scr, c1_scr)
            consume(u, s0_scr, c0_scr)

        q_next = jnp.minimum(qi + 1, n_q - 1)
        build_queries(q_next)
        scores(q_next, 0, s2_scr, c2_scr, False)
        acc = acc_scr[...]
        a = acc[0:HEAD_DV] / acc[HEAD_DV:HEAD_DV + 1]
        out = (a[:, :tq] - lam * a[:, tq:]).T
        q_rows = pl.ds(pl.multiple_of(qi * tq, tq), tq)
        o_ref[0, q_rows, :] = (_rms(out, g_ref[...]) * (1.0 - lam_init)).astype(BF16)
        return carry

    lax.fori_loop(0, n_q, q_tile, 0)


def _attn(proj3, lamqk, g, lam_init):
    b, s, _ = proj3.shape
    qb, kb, vb = OFF_A // LANES, (OFF_A + QK_A) // LANES, (OFF_A + 2 * QK_A) // LANES
    seq = lambda blk: pl.BlockSpec((1, s, LANES), lambda bi, h: (bi, 0, blk + h))
    return pl.pallas_call(
        functools.partial(_attn_kernel, lam_init=lam_init),
        grid=(b, N_HEADS),
        in_specs=[
            pl.BlockSpec((4, DA_HEAD), lambda bi, h: (0, 0)),
            pl.BlockSpec((1, HEAD_DV), lambda bi, h: (0, 0)),
            seq(qb), seq(kb), seq(vb),
        ],
        out_specs=seq(0),
        out_shape=jax.ShapeDtypeStruct((b, s, V_W), BF16),
        scratch_shapes=[
            pltpu.VMEM((2 * TQ_ATT, 2 * DA_HEAD), BF16),
            pltpu.VMEM((s // TK_ATT, VT_ROWS, TK_ATT), BF16),
            pltpu.VMEM((TK_ATT, 2 * TQ_ATT), F32),
            pltpu.VMEM((TK_ATT, 2 * TQ_ATT), F32),
            pltpu.VMEM((TK_ATT, 2 * TQ_ATT), F32),
            pltpu.VMEM((1, 2 * TQ_ATT), F32),
            pltpu.VMEM((1, 2 * TQ_ATT), F32),
            pltpu.VMEM((1, 2 * TQ_ATT), F32),
            pltpu.VMEM((1, 2 * TQ_ATT), F32),
            pltpu.VMEM((VT_ROWS, 2 * TQ_ATT), F32),
        ],
        compiler_params=pltpu.CompilerParams(
            dimension_semantics=("parallel", "parallel"), vmem_limit_bytes=VMEM_LIMIT),
        name="diff_attn",
    )(lamqk, g, proj3, proj3, proj3)


def _stack_heads(t):
    lane = lax.broadcasted_iota(jnp.int32, t.shape, 1)
    return jnp.concatenate(
        [jnp.where((lane // LIN_DK) % N_HEADS == h, t, 0.0) for h in range(N_HEADS)],
        axis=0).astype(BF16)


def _head_rms(o, g):
    return jnp.concatenate(
        [_rms(o[:, HEAD_DV * h:HEAD_DV * (h + 1)], g) for h in range(N_HEADS)], axis=1)


def _swap_halves(t):
    lane = lax.broadcasted_iota(jnp.int32, t.shape, 1)
    half = LIN_DK // 2
    return jnp.where((lane % LIN_DK) < half,
                     pltpu.roll(t, QK_L - half, 1), pltpu.roll(t, half, 1))


def _lin_tile(q, k, la, v_ref, gate_ref, gn_ref, o_ref, st_scr):
    c = CHUNK
    n_chunks = T_LIN // c
    sl = [slice(i * c, (i + 1) * c) for i in range(n_chunks)]
    r = lax.broadcasted_iota(jnp.int32, (c, c), 0)
    cc = lax.broadcasted_iota(jnp.int32, (c, c), 1)
    tri = jnp.where(cc <= r, 1.0, 0.0).astype(BF16)
    cum = []
    for s in sl:
        la_hi = la[s].astype(BF16)
        la_lo = (la[s] - la_hi.astype(F32)).astype(BF16)
        cum.append(_dot(tri, la_hi) + _dot(tri, la_lo))

    row_blk = lax.broadcasted_iota(jnp.int32, (c, QK_L), 0) // SUB
    qb, kb = [], []
    for i, s in enumerate(sl):
        q_parts, k_parts = [], []
        for j in range(c // SUB):
            ref = cum[i][SUB * (j + 1) - 1:SUB * (j + 1)]
            q_parts.append(jnp.where(row_blk >= j, q[s] * jnp.exp(cum[i] - ref), 0.0))
            k_parts.append(jnp.where(row_blk == j, k[s] * jnp.exp(ref - cum[i]), 0.0))
        qb.append(jnp.concatenate(q_parts, axis=1).astype(BF16))
        kb.append(_stack_heads(jnp.concatenate(k_parts, axis=1)))

    srow = lax.broadcasted_iota(jnp.int32, (c, N_HEADS * c), 0)
    scol = lax.broadcasted_iota(jnp.int32, (c, N_HEADS * c), 1)
    causal = scol % c <= srow
    sc = [jnp.where(causal, _dot_nt(qb[i], kb[i]), 0.0).astype(BF16) for i in range(n_chunks)]

    vlane = lax.broadcasted_iota(jnp.int32, (1, V_W), 1) // HEAD_DV
    vmask = [jnp.where(vlane == h, 1.0, 0.0).astype(BF16) for h in range(N_HEADS)]
    v = [v_ref[0, s, :] for s in sl]
    inner = [_dot(sc[i], jnp.concatenate([v[i] * vmask[h] for h in range(N_HEADS)], axis=0))
             for i in range(n_chunks)]

    last = [cu[c - 1:c] for cu in cum]
    upd = []
    for i, s in enumerate(sl):
        ke = _stack_heads(k[s] * jnp.exp(last[i] - cum[i]))
        vs = jnp.concatenate(
            [v[i][:, HEAD_DV * h:HEAD_DV * (h + 1)] for h in range(N_HEADS)], axis=0)
        upd.append(_dot_tn(vs, ke))

    st = st_scr[...]
    slane = lax.broadcasted_iota(jnp.int32, st.shape, 1) // LIN_DK
    outs = []
    for i, s in enumerate(sl):
        st_bd = jnp.concatenate(
            [jnp.where(slane == h, st, 0.0) for h in range(N_HEADS)], axis=0).astype(BF16)
        cross = _dot_nt((q[s] * jnp.exp(cum[i])).astype(BF16), st_bd)
        outs.append(inner[i] + cross)
        st = st * jnp.exp(last[i]) + upd[i]
    st_scr[...] = st
    o = jnp.concatenate(outs, axis=0)
    y = jax.nn.silu(gate_ref[0].astype(F32)) * _head_rms(o, gn_ref[...])
    o_ref[0] = y.astype(BF16)


def _ret_kernel(q_ref, k_ref, v_ref, gate_ref, cos_ref, sin_ref, lg_ref, gn_ref, o_ref, st_scr):
    @pl.when(pl.program_id(1) == 0)
    def _():
        st_scr[...] = jnp.zeros_like(st_scr)

    cos = cos_ref[...]
    sin = sin_ref[...]
    q = q_ref[0].astype(F32)
    k = k_ref[0].astype(F32)
    q = q * cos + _swap_halves(q) * sin
    k = (k * cos + _swap_halves(k) * sin) * (LIN_DK ** -0.5)
    la = jnp.broadcast_to(lg_ref[...], (T_LIN, QK_L))
    _lin_tile(q, k, la, v_ref, gate_ref, gn_ref, o_ref, st_scr)


def _gla_kernel(q_ref, k_ref, v_ref, gate_ref, ca_ref, a2_ref, ab_ref, gn_ref, o_ref, st_scr):
    @pl.when(pl.program_id(1) == 0)
    def _():
        st_scr[...] = jnp.zeros_like(st_scr)

    z = _dot(ca_ref[0].astype(BF16), a2_ref[...]) + ab_ref[...]
    la = (jnp.minimum(z, 0.0) - jnp.log(1.0 + jnp.exp(-jnp.abs(z)))) * (1.0 / GLA_TAU)
    q = q_ref[0].astype(F32) * (LIN_DK ** -0.5)
    k = k_ref[0].astype(F32)
    _lin_tile(q, k, la, v_ref, gate_ref, gn_ref, o_ref, st_scr)


def _lin_specs(off):
    qb, kb = off // QK_L, (off + QK_L) // QK_L
    vb, gb = (off + 2 * QK_L) // V_W, (off + 2 * QK_L + V_W) // V_W
    return [
        pl.BlockSpec((1, T_LIN, QK_L), lambda bi, t: (bi, t, qb)),
        pl.BlockSpec((1, T_LIN, QK_L), lambda bi, t: (bi, t, kb)),
        pl.BlockSpec((1, T_LIN, V_W), lambda bi, t: (bi, t, vb)),
        pl.BlockSpec((1, T_LIN, V_W), lambda bi, t: (bi, t, gb)),
    ]


def _lin_call(kern, name, proj3, extra_specs, extra_args, off):
    b, s, _ = proj3.shape
    return pl.pallas_call(
        kern,
        grid=(b, s // T_LIN),
        in_specs=_lin_specs(off) + extra_specs,
        out_specs=pl.BlockSpec((1, T_LIN, V_W), lambda bi, t: (bi, t, 0)),
        out_shape=jax.ShapeDtypeStruct((b, s, V_W), BF16),
        scratch_shapes=[pltpu.VMEM((HEAD_DV, QK_L), F32)],
        compiler_params=pltpu.CompilerParams(
            dimension_semantics=("parallel", "arbitrary"), vmem_limit_bytes=VMEM_LIMIT),
        name=name,
    )(proj3, proj3, proj3, proj3, *extra_args)


def _retention(proj3, cos, sin, lg, gn):
    extra = [
        pl.BlockSpec((T_LIN, QK_L), lambda bi, t: (t, 0)),
        pl.BlockSpec((T_LIN, QK_L), lambda bi, t: (t, 0)),
        pl.BlockSpec((1, QK_L), lambda bi, t: (0, 0)),
        pl.BlockSpec((1, HEAD_DV), lambda bi, t: (0, 0)),
    ]
    return _lin_call(_ret_kernel, "retention", proj3, extra, (cos, sin, lg, gn), OFF_B)


def _gla(proj3, ca3, a2, ab, gn):
    extra = [
        pl.BlockSpec((1, T_LIN, LANES), lambda bi, t: (bi, t, 0)),
        pl.BlockSpec((LANES, QK_L), lambda bi, t: (0, 0)),
        pl.BlockSpec((1, QK_L), lambda bi, t: (0, 0)),
        pl.BlockSpec((1, HEAD_DV), lambda bi, t: (0, 0)),
    ]
    return _lin_call(_gla_kernel, "gla", proj3, extra, (ca3, a2, ab, gn), OFF_C)


def _merge_kernel(x_ref, g0_ref, g1_ref, g2_ref, ya_ref, yb_ref, yc_ref,
                  wa_ref, wb_ref, wc_ref, wo_ref, o_ref):
    def sig(ref):
        return jax.nn.sigmoid(ref[...].astype(F32))

    merged = (sig(g0_ref) * _dot(ya_ref[...], wa_ref[...])
              + sig(g1_ref) * _dot(yb_ref[...], wb_ref[...])
              + sig(g2_ref) * _dot(yc_ref[...], wc_ref[...]))
    o_ref[...] = x_ref[...] + _dot(merged.astype(BF16), wo_ref[...])


def _merge(x2, proj2, ya, yb, yc, wa, wb, wc, wo):
    n = x2.shape[0]
    row = lambda w: pl.BlockSpec((TM_MERGE, w), lambda i: (i, 0))
    full = lambda a: pl.BlockSpec(a.shape, lambda i: (0, 0))
    return pl.pallas_call(
        _merge_kernel,
        grid=(n // TM_MERGE,),
        in_specs=[
            row(D_MODEL),
            pl.BlockSpec((TM_MERGE, D_MODEL), lambda i: (i, 0)),
            pl.BlockSpec((TM_MERGE, D_MODEL), lambda i: (i, 1)),
            pl.BlockSpec((TM_MERGE, D_MODEL), lambda i: (i, 2)),
            row(V_W), row(V_W), row(V_W),
            full(wa), full(wb), full(wc), full(wo),
        ],
        out_specs=row(D_MODEL),
        out_shape=jax.ShapeDtypeStruct((n, D_MODEL), F32),
        compiler_params=pltpu.CompilerParams(
            dimension_semantics=("parallel",), vmem_limit_bytes=VMEM_LIMIT),
        name="merge",
    )(x2, proj2, proj2, proj2, ya, yb, yc, wa, wb, wc, wo)


def _rope_tables(s):
    half = LIN_DK // 2
    inv = ROPE_BASE ** (-jnp.arange(0, LIN_DK, 2, dtype=F32) / LIN_DK)
    ang = jnp.arange(s, dtype=F32)[:, None] * inv[None, :]
    cos, sin = jnp.cos(ang), jnp.sin(ang)
    del half
    cos_t = jnp.tile(jnp.concatenate([cos, cos], axis=-1), (1, N_HEADS))
    sin_t = jnp.tile(jnp.concatenate([-sin, sin], axis=-1), (1, N_HEADS))
    return cos_t, sin_t


def kernel(x, ffn1_norm, ffn1_w1, ffn1_w3, ffn1_w2, mix_norm, w_in, lam_qk, da_norm, ret_norm,
           gla_a2, gla_a_bias, gla_norm, w_branch_a, w_branch_b, w_branch_c, w_out,
           ffn2_norm, ffn2_w1, ffn2_w3, ffn2_w2, final_norm):
    b, s, d = x.shape
    assert d == D_MODEL and s % T_LIN == 0 and s % TK_ATT == 0 and TK_ATT % TQ_ATT == 0
    assert (b * s) % TM_FFN == 0
    pad_f = D_FF_PAD - D_FF

    def prep_ffn(w1, w3, w2):
        return (jnp.pad(w1, ((0, 0), (0, 0), (0, pad_f))).astype(BF16),
                jnp.pad(w3, ((0, 0), (0, 0), (0, pad_f))).astype(BF16),
                jnp.pad(w2, ((0, 0), (0, pad_f), (0, 0))).astype(BF16))

    f1 = prep_ffn(ffn1_w1, ffn1_w3, ffn1_w2)
    f2 = prep_ffn(ffn2_w1, ffn2_w3, ffn2_w2)

    n_abc = OFF_C + 2 * QK_L + 2 * V_W - GATE_W
    w_main = jnp.concatenate(
        [w_in[:, :, n_abc + GLA_RANK:], w_in[:, :, :n_abc]], axis=-1).astype(BF16)
    w_ca = jnp.pad(w_in[:, :, n_abc:n_abc + GLA_RANK],
                   ((0, 0), (0, 0), (0, LANES - GLA_RANK))).astype(BF16)
    a2 = jnp.pad(gla_a2, ((0, 0), (0, LANES - GLA_RANK), (0, 0))).astype(BF16)
    wa, wb, wc, wo = (w.astype(BF16) for w in (w_branch_a, w_branch_b, w_branch_c, w_out))

    cos_t, sin_t = _rope_tables(s)
    log_g = jnp.log1p(-jnp.exp2(-5.0 - jnp.arange(N_HEADS, dtype=F32)))
    lg = jnp.repeat(log_g, LIN_DK)[None, :]
    fin = final_norm[None, :]

    x2 = x.reshape(b * s, d)
    for l in range(DEPTH):
        lam_init = 0.8 - 0.6 * math.exp(-0.3 * l)
        x2 = _ffn(x2, ffn1_norm[l][None, :], f1[0][l], f1[1][l], f1[2][l], fin, False)
        proj2, ca2 = _inproj(x2, mix_norm[l][None, :], w_main[l], w_ca[l])
        proj3 = proj2.reshape(b, s, PROJ_W)
        ya = _attn(proj3, lam_qk[l], da_norm[l][None, :], lam_init)
        yb = _retention(proj3, cos_t, sin_t, lg, ret_norm[l][None, :])
        yc = _gla(proj3, ca2.reshape(b, s, LANES), a2[l], gla_a_bias[l][None, :], gla_norm[l][None, :])
        x2 = _merge(x2, proj2, ya.reshape(b * s, V_W), yb.reshape(b * s, V_W), yc.reshape(b * s, V_W),
                    wa[l], wb[l], wc[l], wo[l])
        x2 = _ffn(x2, ffn2_norm[l][None, :], f2[0][l], f2[1][l], f2[2][l], fin, l == DEPTH - 1)
    return x2.reshape(b, s, d)
```

```python
import functools
import math

import jax
import jax.numpy as jnp
from jax import lax
from jax.experimental import pallas as pl
from jax.experimental.pallas import tpu as pltpu

F32 = jnp.float32
BF16 = jnp.bfloat16

D_MODEL = 1024
N_HEADS = 4
DEPTH = 4
CHUNK = 64
DA_HEAD = 64
HEAD_DV = 128
LIN_DK = 64
GLA_RANK = 16
GLA_TAU = 16.0
ROPE_BASE = 10000.0
D_FF = 2752
EPS = 1e-6

LANES = 128
MXU_DIM = 256

D_FF_PAD = ((D_FF + MXU_DIM - 1) // MXU_DIM) * MXU_DIM
QK_A = N_HEADS * 2 * DA_HEAD
V_W = N_HEADS * HEAD_DV
QK_L = N_HEADS * LIN_DK
GATE_W = 3 * D_MODEL
OFF_A = GATE_W
OFF_B = OFF_A + 2 * QK_A + V_W
OFF_C = OFF_B + 2 * QK_L + 2 * V_W
PROJ_W = OFF_C + 2 * QK_L + 2 * V_W

TM_FFN = 512
TF_FFN = MXU_DIM
NF_FFN = D_FF_PAD // TF_FFN
TM_PROJ = 512
TN_PROJ = 512
TQ_ATT = 512
TK_ATT = 512
VT_ROWS = HEAD_DV + 16
T_LIN = 1024
SUB = 16
TM_MERGE = 512
VMEM_LIMIT = 48 * 1024 * 1024


def _rms(xf, g):
    return xf * lax.rsqrt(jnp.mean(xf * xf, axis=-1, keepdims=True) + EPS) * g


def _dot(a, b):
    return jnp.dot(a, b, preferred_element_type=F32)


def _dot_nt(a, b):
    return lax.dot_general(a, b, (((1,), (1,)), ((), ())), preferred_element_type=F32)


def _dot_tn(a, b):
    return lax.dot_general(a, b, (((0,), (0,)), ((), ())), preferred_element_type=F32)


def _ffn_kernel(x_ref, g_ref, w1_ref, w3_ref, w2_ref, gf_ref, o_ref, *, final):
    x = x_ref[...]
    h = _rms(x, g_ref[...]).astype(BF16)
    acc = None
    for c in range(NF_FFN):
        cs = slice(c * TF_FFN, (c + 1) * TF_FFN)
        a = _dot(h, w1_ref[:, cs])
        b = _dot(h, w3_ref[:, cs])
        t = (a * jax.nn.sigmoid(a) * b).astype(BF16)
        d = _dot(t, w2_ref[cs, :])
        acc = d if acc is None else acc + d
    y = x + 0.5 * acc
    if final:
        y = _rms(y, gf_ref[...])
    o_ref[...] = y


def _resident(shape):
    return pl.BlockSpec(shape, lambda i: (0,) * len(shape), pipeline_mode=pl.Buffered(1))


def _ffn(x2, g, w1, w3, w2, gf, final):
    n = x2.shape[0]
    return pl.pallas_call(
        functools.partial(_ffn_kernel, final=final),
        grid=(n // TM_FFN,),
        in_specs=[
            pl.BlockSpec((TM_FFN, D_MODEL), lambda i: (i, 0)),
            _resident((1, D_MODEL)),
            _resident(w1.shape), _resident(w3.shape), _resident(w2.shape),
            _resident((1, D_MODEL)),
        ],
        out_specs=pl.BlockSpec((TM_FFN, D_MODEL), lambda i: (i, 0)),
        out_shape=jax.ShapeDtypeStruct((n, D_MODEL), F32),
        compiler_params=pltpu.CompilerParams(
            dimension_semantics=("parallel",), vmem_limit_bytes=VMEM_LIMIT),
        name="ffn",
    )(x2, g, w1, w3, w2, gf)


def _inproj_kernel(x_ref, g_ref, w_ref, wca_ref, proj_ref, ca_ref):
    h = _rms(x_ref[...], g_ref[...]).astype(BF16)
    ca_ref[...] = _dot(h, wca_ref[...])
    for c in range(PROJ_W // TN_PROJ):
        cs = slice(c * TN_PROJ, (c + 1) * TN_PROJ)
        proj_ref[:, cs] = _dot(h, w_ref[:, cs]).astype(BF16)


def _inproj(x2, g, w, wca):
    n = x2.shape[0]
    return pl.pallas_call(
        _inproj_kernel,
        grid=(n // TM_PROJ,),
        in_specs=[
            pl.BlockSpec((TM_PROJ, D_MODEL), lambda i: (i, 0)),
            _resident((1, D_MODEL)),
            _resident(w.shape),
            _resident(wca.shape),
        ],
        out_specs=[
            pl.BlockSpec((TM_PROJ, PROJ_W), lambda i: (i, 0)),
            pl.BlockSpec((TM_PROJ, LANES), lambda i: (i, 0)),
        ],
        out_shape=[
            jax.ShapeDtypeStruct((n, PROJ_W), BF16),
            jax.ShapeDtypeStruct((n, LANES), F32),
        ],
        compiler_params=pltpu.CompilerParams(
            dimension_semantics=("parallel",), vmem_limit_bytes=VMEM_LIMIT),
        name="inproj",
    )(x2, g, w, wca)


def _attn_kernel(lamqk_ref, g_ref, q_ref, k_ref, v_ref, o_ref,
                 qq_scr, vt_scr, s0_scr, s1_scr, s2_scr, c0_scr, c1_scr, c2_scr, m_scr, acc_scr,
                 *, lam_init):
    tq, tk = TQ_ATT, TK_ATT

    extra = jnp.where(lax.broadcasted_iota(jnp.int32, (VT_ROWS - HEAD_DV, tk), 0) == 0, 1.0, 0.0)
    eye = jnp.where(lax.broadcasted_iota(jnp.int32, (HEAD_DV, HEAD_DV), 0)
                    == lax.broadcasted_iota(jnp.int32, (HEAD_DV, HEAD_DV), 1), 1.0, 0.0).astype(BF16)
    for t in range(vt_scr.shape[0]):
        vt_scr[t, 0:HEAD_DV, :] = _dot_nt(eye, v_ref[0, t * tk:(t + 1) * tk, :]).astype(BF16)
        vt_scr[t, HEAD_DV:VT_ROWS, :] = extra.astype(BF16)

    lq = lamqk_ref[...]
    lam = (jnp.exp(jnp.sum(lq[0:1] * lq[1:2], axis=-1, keepdims=True))
           - jnp.exp(jnp.sum(lq[2:3] * lq[3:4], axis=-1, keepdims=True)) + lam_init)

    def build_queries(qi):
        q_rows = pl.ds(pl.multiple_of(qi * tq, tq), tq)
        q = q_ref[0, q_rows, :].astype(F32) * (DA_HEAD ** -0.5 * math.log2(math.e))
        lane = lax.broadcasted_iota(jnp.int32, (tq, 2 * DA_HEAD), 1)
        qq_scr[...] = jnp.concatenate(
            [jnp.where(lane < DA_HEAD, q, 0.0), jnp.where(lane >= DA_HEAD, q, 0.0)], axis=0).astype(BF16)

    def scores_half(qi, j, s_ref, c_ref, masked, hf):
        start = pl.multiple_of(j * tk, tk)
        cols = slice(hf * tq, (hf + 1) * tq)
        s = _dot_nt(k_ref[0, pl.ds(start, tk), :], qq_scr[cols, :])
        if masked:
            key_chunk = (j * tk + lax.broadcasted_iota(jnp.int32, (tk, 1), 0)) // CHUNK
            q_chunk = (qi * tq + lax.broadcasted_iota(jnp.int32, (1, tq), 1)) // CHUNK
            s = jnp.where(key_chunk <= q_chunk, s, -jnp.inf)
        s_ref[:, cols] = s
        c_ref[:, cols] = jnp.max(s, axis=0, keepdims=True)

    def consume_half(j, s_ref, c_ref, hf):
        cols = slice(hf * tq, (hf + 1) * tq)
        m_prev = m_scr[:, cols]
        m_new = jnp.maximum(m_prev, c_ref[:, cols])
        alpha = jnp.exp2(m_prev - m_new)
        p = jnp.exp2(s_ref[:, cols] - m_new).astype(BF16)
        acc_scr[:, cols] = alpha * acc_scr[:, cols] + _dot(vt_scr[j], p)
        m_scr[:, cols] = m_new

    def scores(qi, j, s_ref, c_ref, masked):
        for hf in (0, 1):
            scores_half(qi, j, s_ref, c_ref, masked, hf)

    def consume(j, s_ref, c_ref):
        for hf in (0, 1):
            consume_half(j, s_ref, c_ref, hf)

    def overlapped(qi, j_next, s_next, c_next, masked, j_cur, s_cur, c_cur):
        for hf in (0, 1):
            scores_half(qi, j_next, s_next, c_next, masked, hf)
            consume_half(j_cur, s_cur, c_cur, hf)

    n_q = q_ref.shape[1] // tq
    build_queries(0)
    scores(0, 0, s2_scr, c2_scr, True)

    def q_tile(qi, carry):
        m_scr[...] = jnp.full(m_scr.shape, -jnp.inf, F32)
        acc_scr[...] = jnp.zeros_like(acc_scr)
        u = qi

        @pl.when(u == 0)
        def _():
            consume(0, s2_scr, c2_scr)

        @pl.when(u == 1)
        def _():
            overlapped(qi, 1, s1_scr, c1_scr, True, 0, s2_scr, c2_scr)
            consume(1, s1_scr, c1_scr)

        @pl.when(u > 1)
        def _():
            overlapped(qi, 1, s1_scr, c1_scr, False, 0, s2_scr, c2_scr)

        w = u - 1

        def pair(i, c):
            overlapped(qi, 2 * i + 2, s0_scr, c0_scr, False, 2 * i + 1, s1_scr, c1_scr)
            overlapped(qi, 2 * i + 3, s1_scr, c1_scr, False, 2 * i + 2, s0_scr, c0_scr)
            return c

        n_pairs = jnp.where(w % 2 == 0, jnp.maximum(w // 2 - 1, 0), jnp.maximum(w, 0) // 2)
        def quad(i, c):
            pair(2 * i, c)
            pair(2 * i + 1, c)
            return c

        lax.fori_loop(0, n_pairs // 2, quad, 0)

        @pl.when(n_pairs % 2 == 1)
        def _():
            pair(n_pairs - 1, 0)

        @pl.when(jnp.logical_and(w % 2 == 0, w > 0))
        def _():
            overlapped(qi, u - 1, s0_scr, c0_scr, False, u - 2, s1_scr, c1_scr)
            overlapped(qi, u, s1_scr, c1_scr, True, u - 1, s0_scr, c0_scr)
            consume(u, s1_scr, c1_scr)

        @pl.when(jnp.logical_and(w % 2 == 1, w > 0))
        def _():
            overlapped(qi, u, s0_scr, c0_scr, True, u - 1, s1_scr, c1_scr)
            consume(u, s0_scr, c0_scr)

        q_next = jnp.minimum(qi + 1, n_q - 1)
        build_queries(q_next)
        scores(q_next, 0, s2_scr, c2_scr, False)
        acc = acc_scr[...]
        a = acc[0:HEAD_DV] / acc[HEAD_DV:HEAD_DV + 1]
        out = (a[:, :tq] - lam * a[:, tq:]).T
        q_rows = pl.ds(pl.multiple_of(qi * tq, tq), tq)
        o_ref[0, q_rows, :] = (_rms(out, g_ref[...]) * (1.0 - lam_init)).astype(BF16)
        return carry

    lax.fori_loop(0, n_q, q_tile, 0)


def _attn(proj3, lamqk, g, lam_init):
    b, s, _ = proj3.shape
    qb, kb, vb = OFF_A // LANES, (OFF_A + QK_A) // LANES, (OFF_A + 2 * QK_A) // LANES
    seq = lambda blk: pl.BlockSpec((1, s, LANES), lambda bi, h: (bi, 0, blk + h))
    return pl.pallas_call(
        functools.partial(_attn_kernel, lam_init=lam_init),
        grid=(b, N_HEADS),
        in_specs=[
            pl.BlockSpec((4, DA_HEAD), lambda bi, h: (0, 0)),
            pl.BlockSpec((1, HEAD_DV), lambda bi, h: (0, 0)),
            seq(qb), seq(kb), seq(vb),
        ],
        out_specs=seq(0),
        out_shape=jax.ShapeDtypeStruct((b, s, V_W), BF16),
        scratch_shapes=[
            pltpu.VMEM((2 * TQ_ATT, 2 * DA_HEAD), BF16),
            pltpu.VMEM((s // TK_ATT, VT_ROWS, TK_ATT), BF16),
            pltpu.VMEM((TK_ATT, 2 * TQ_ATT), F32),
            pltpu.VMEM((TK_ATT, 2 * TQ_ATT), F32),
            pltpu.VMEM((TK_ATT, 2 * TQ_ATT), F32),
            pltpu.VMEM((1, 2 * TQ_ATT), F32),
            pltpu.VMEM((1, 2 * TQ_ATT), F32),
            pltpu.VMEM((1, 2 * TQ_ATT), F32),
            pltpu.VMEM((1, 2 * TQ_ATT), F32),
            pltpu.VMEM((VT_ROWS, 2 * TQ_ATT), F32),
        ],
        compiler_params=pltpu.CompilerParams(
            dimension_semantics=("parallel", "parallel"), vmem_limit_bytes=VMEM_LIMIT),
        name="diff_attn",
    )(lamqk, g, proj3, proj3, proj3)


def _stack_heads(t):
    lane = lax.broadcasted_iota(jnp.int32, t.shape, 1)
    return jnp.concatenate(
        [jnp.where((lane // LIN_DK) % N_HEADS == h, t, 0.0) for h in range(N_HEADS)],
        axis=0).astype(BF16)


def _head_rms(o, g):
    return jnp.concatenate(
        [_rms(o[:, HEAD_DV * h:HEAD_DV * (h + 1)], g) for h in range(N_HEADS)], axis=1)


def _swap_halves(t):
    lane = lax.broadcasted_iota(jnp.int32, t.shape, 1)
    half = LIN_DK // 2
    return jnp.where((lane % LIN_DK) < half,
                     pltpu.roll(t, QK_L - half, 1), pltpu.roll(t, half, 1))


def _lin_tile(q, k, la, v_ref, gate_ref, gn_ref, o_ref, st_scr):
    c = CHUNK
    n_chunks = T_LIN // c
    sl = [slice(i * c, (i + 1) * c) for i in range(n_chunks)]
    r = lax.broadcasted_iota(jnp.int32, (c, c), 0)
    cc = lax.broadcasted_iota(jnp.int32, (c, c), 1)
    tri = jnp.where(cc <= r, 1.0, 0.0).astype(BF16)
    cum = []
    for s in sl:
        la_hi = la[s].astype(BF16)
        la_lo = (la[s] - la_hi.astype(F32)).astype(BF16)
        cum.append(_dot(tri, la_hi) + _dot(tri, la_lo))

    row_blk = lax.broadcasted_iota(jnp.int32, (c, QK_L), 0) // SUB
    qb, kb = [], []
    for i, s in enumerate(sl):
        q_parts, k_parts = [], []
        for j in range(c // SUB):
            ref = cum[i][SUB * (j + 1) - 1:SUB * (j + 1)]
            q_parts.append(jnp.where(row_blk >= j, q[s] * jnp.exp(cum[i] - ref), 0.0))
            k_parts.append(jnp.where(row_blk == j, k[s] * jnp.exp(ref - cum[i]), 0.0))
        qb.append(jnp.concatenate(q_parts, axis=1).astype(BF16))
        kb.append(_stack_heads(jnp.concatenate(k_parts, axis=1)))

    srow = lax.broadcasted_iota(jnp.int32, (c, N_HEADS * c), 0)
    scol = lax.broadcasted_iota(jnp.int32, (c, N_HEADS * c), 1)
    causal = scol % c <= srow
    sc = [jnp.where(causal, _dot_nt(qb[i], kb[i]), 0.0).astype(BF16) for i in range(n_chunks)]

    vlane = lax.broadcasted_iota(jnp.int32, (1, V_W), 1) // HEAD_DV
    vmask = [jnp.where(vlane == h, 1.0, 0.0).astype(BF16) for h in range(N_HEADS)]
    v = [v_ref[0, s, :] for s in sl]
    inner = [_dot(sc[i], jnp.concatenate([v[i] * vmask[h] for h in range(N_HEADS)], axis=0))
             for i in range(n_chunks)]

    last = [cu[c - 1:c] for cu in cum]
    upd = []
    for i, s in enumerate(sl):
        ke = _stack_heads(k[s] * jnp.exp(last[i] - cum[i]))
        vs = jnp.concatenate(
            [v[i][:, HEAD_DV * h:HEAD_DV * (h + 1)] for h in range(N_HEADS)], axis=0)
        upd.append(_dot_tn(vs, ke))

    st = st_scr[...]
    slane = lax.broadcasted_iota(jnp.int32, st.shape, 1) // LIN_DK
    outs = []
    for i, s in enumerate(sl):
        st_bd = jnp.concatenate(
            [jnp.where(slane == h, st, 0.0) for h in range(N_HEADS)], axis=0).astype(BF16)
        cross = _dot_nt((q[s] * jnp.exp(cum[i])).astype(BF16), st_bd)
        outs.append(inner[i] + cross)
        st = st * jnp.exp(last[i]) + upd[i]
    st_scr[...] = st
    o = jnp.concatenate(outs, axis=0)
    y = jax.nn.silu(gate_ref[0].astype(F32)) * _head_rms(o, gn_ref[...])
    o_ref[0] = y.astype(BF16)


def _ret_kernel(q_ref, k_ref, v_ref, gate_ref, cos_ref, sin_ref, lg_ref, gn_ref, o_ref, st_scr):
    @pl.when(pl.program_id(1) == 0)
    def _():
        st_scr[...] = jnp.zeros_like(st_scr)

    cos = cos_ref[...]
    sin = sin_ref[...]
    q = q_ref[0].astype(F32)
    k = k_ref[0].astype(F32)
    q = q * cos + _swap_halves(q) * sin
    k = (k * cos + _swap_halves(k) * sin) * (LIN_DK ** -0.5)
    la = jnp.broadcast_to(lg_ref[...], (T_LIN, QK_L))
    _lin_tile(q, k, la, v_ref, gate_ref, gn_ref, o_ref, st_scr)


def _gla_kernel(q_ref, k_ref, v_ref, gate_ref, ca_ref, a2_ref, ab_ref, gn_ref, o_ref, st_scr):
    @pl.when(pl.program_id(1) == 0)
    def _():
        st_scr[...] = jnp.zeros_like(st_scr)

    z = _dot(ca_ref[0].astype(BF16), a2_ref[...]) + ab_ref[...]
    la = (jnp.minimum(z, 0.0) - jnp.log(1.0 + jnp.exp(-jnp.abs(z)))) * (1.0 / GLA_TAU)
    q = q_ref[0].astype(F32) * (LIN_DK ** -0.5)
    k = k_ref[0].astype(F32)
    _lin_tile(q, k, la, v_ref, gate_ref, gn_ref, o_ref, st_scr)


def _lin_specs(off):
    qb, kb = off // QK_L, (off + QK_L) // QK_L
    vb, gb = (off + 2 * QK_L) // V_W, (off + 2 * QK_L + V_W) // V_W
    return [
        pl.BlockSpec((1, T_LIN, QK_L), lambda bi, t: (bi, t, qb)),
        pl.BlockSpec((1, T_LIN, QK_L), lambda bi, t: (bi, t, kb)),
        pl.BlockSpec((1, T_LIN, V_W), lambda bi, t: (bi, t, vb)),
        pl.BlockSpec((1, T_LIN, V_W), lambda bi, t: (bi, t, gb)),
    ]


def _lin_call(kern, name, proj3, extra_specs, extra_args, off):
    b, s, _ = proj3.shape
    return pl.pallas_call(
        kern,
        grid=(b, s // T_LIN),
        in_specs=_lin_specs(off) + extra_specs,
        out_specs=pl.BlockSpec((1, T_LIN, V_W), lambda bi, t: (bi, t, 0)),
        out_shape=jax.ShapeDtypeStruct((b, s, V_W), BF16),
        scratch_shapes=[pltpu.VMEM((HEAD_DV, QK_L), F32)],
        compiler_params=pltpu.CompilerParams(
            dimension_semantics=("parallel", "arbitrary"), vmem_limit_bytes=VMEM_LIMIT),
        name=name,
    )(proj3, proj3, proj3, proj3, *extra_args)


def _retention(proj3, cos, sin, lg, gn):
    extra = [
        pl.BlockSpec((T_LIN, QK_L), lambda bi, t: (t, 0)),
        pl.BlockSpec((T_LIN, QK_L), lambda bi, t: (t, 0)),
        pl.BlockSpec((1, QK_L), lambda bi, t: (0, 0)),
        pl.BlockSpec((1, HEAD_DV), lambda bi, t: (0, 0)),
    ]
    return _lin_call(_ret_kernel, "retention", proj3, extra, (cos, sin, lg, gn), OFF_B)


def _gla(proj3, ca3, a2, ab, gn):
    extra = [
        pl.BlockSpec((1, T_LIN, LANES), lambda bi, t: (bi, t, 0)),
        pl.BlockSpec((LANES, QK_L), lambda bi, t: (0, 0)),
        pl.BlockSpec((1, QK_L), lambda bi, t: (0, 0)),
        pl.BlockSpec((1, HEAD_DV), lambda bi, t: (0, 0)),
    ]
    return _lin_call(_gla_kernel, "gla", proj3, extra, (ca3, a2, ab, gn), OFF_C)


def _merge_kernel(x_ref, g0_ref, g1_ref, g2_ref, ya_ref, yb_ref, yc_ref,
                  wa_ref, wb_ref, wc_ref, wo_ref, o_ref):
    def sig(ref):
        return jax.nn.sigmoid(ref[...].astype(F32))

    merged = (sig(g0_ref) * _dot(ya_ref[...], wa_ref[...])
              + sig(g1_ref) * _dot(yb_ref[...], wb_ref[...])
              + sig(g2_ref) * _dot(yc_ref[...], wc_ref[...]))
    o_ref[...] = x_ref[...] + _dot(merged.astype(BF16), wo_ref[...])


def _merge(x2, proj2, ya, yb, yc, wa, wb, wc, wo):
    n = x2.shape[0]
    row = lambda w: pl.BlockSpec((TM_MERGE, w), lambda i: (i, 0))
    full = lambda a: pl.BlockSpec(a.shape, lambda i: (0, 0))
    return pl.pallas_call(
        _merge_kernel,
        grid=(n // TM_MERGE,),
        in_specs=[
            row(D_MODEL),
            pl.BlockSpec((TM_MERGE, D_MODEL), lambda i: (i, 0)),
            pl.BlockSpec((TM_MERGE, D_MODEL), lambda i: (i, 1)),
            pl.BlockSpec((TM_MERGE, D_MODEL), lambda i: (i, 2)),
            row(V_W), row(V_W), row(V_W),
            full(wa), full(wb), full(wc), full(wo),
        ],
        out_specs=row(D_MODEL),
        out_shape=jax.ShapeDtypeStruct((n, D_MODEL), F32),
        compiler_params=pltpu.CompilerParams(
            dimension_semantics=("parallel",), vmem_limit_bytes=VMEM_LIMIT),
        name="merge",
    )(x2, proj2, proj2, proj2, ya, yb, yc, wa, wb, wc, wo)


def _rope_tables(s):
    half = LIN_DK // 2
    inv = ROPE_BASE ** (-jnp.arange(0, LIN_DK, 2, dtype=F32) / LIN_DK)
    ang = jnp.arange(s, dtype=F32)[:, None] * inv[None, :]
    cos, sin = jnp.cos(ang), jnp.sin(ang)
    del half
    cos_t = jnp.tile(jnp.concatenate([cos, cos], axis=-1), (1, N_HEADS))
    sin_t = jnp.tile(jnp.concatenate([-sin, sin], axis=-1), (1, N_HEADS))
    return cos_t, sin_t


def kernel(x, ffn1_norm, ffn1_w1, ffn1_w3, ffn1_w2, mix_norm, w_in, lam_qk, da_norm, ret_norm,
           gla_a2, gla_a_bias, gla_norm, w_branch_a, w_branch_b, w_branch_c, w_out,
           ffn2_norm, ffn2_w1, ffn2_w3, ffn2_w2, final_norm):
    b, s, d = x.shape
    assert d == D_MODEL and s % T_LIN == 0 and s % TK_ATT == 0 and TK_ATT % TQ_ATT == 0
    assert (b * s) % TM_FFN == 0
    pad_f = D_FF_PAD - D_FF

    def prep_ffn(w1, w3, w2):
        return (jnp.pad(w1, ((0, 0), (0, 0), (0, pad_f))).astype(BF16),
                jnp.pad(w3, ((0, 0), (0, 0), (0, pad_f))).astype(BF16),
                jnp.pad(w2, ((0, 0), (0, pad_f), (0, 0))).astype(BF16))

    f1 = prep_ffn(ffn1_w1, ffn1_w3, ffn1_w2)
    f2 = prep_ffn(ffn2_w1, ffn2_w3, ffn2_w2)

    n_abc = OFF_C + 2 * QK_L + 2 * V_W - GATE_W
    w_main = jnp.concatenate(
        [w_in[:, :, n_abc + GLA_RANK:], w_in[:, :, :n_abc]], axis=-1).astype(BF16)
    w_ca = jnp.pad(w_in[:, :, n_abc:n_abc + GLA_RANK],
                   ((0, 0), (0, 0), (0, LANES - GLA_RANK))).astype(BF16)
    a2 = jnp.pad(gla_a2, ((0, 0), (0, LANES - GLA_RANK), (0, 0))).astype(BF16)
    wa, wb, wc, wo = (w.astype(BF16) for w in (w_branch_a, w_branch_b, w_branch_c, w_out))

    cos_t, sin_t = _rope_tables(s)
    log_g = jnp.log1p(-jnp.exp2(-5.0 - jnp.arange(N_HEADS, dtype=F32)))
    lg = jnp.repeat(log_g, LIN_DK)[None, :]
    fin = final_norm[None, :]

    x2 = x.reshape(b * s, d)
    for l in range(DEPTH):
        lam_init = 0.8 - 0.6 * math.exp(-0.3 * l)
        x2 = _ffn(x2, ffn1_norm[l][None, :], f1[0][l], f1[1][l], f1[2][l], fin, False)
        proj2, ca2 = _inproj(x2, mix_norm[l][None, :], w_main[l], w_ca[l])
        proj3 = proj2.reshape(b, s, PROJ_W)
        ya = _attn(proj3, lam_qk[l], da_norm[l][None, :], lam_init)
        yb = _retention(proj3, cos_t, sin_t, lg, ret_norm[l][None, :])
        yc = _gla(proj3, ca2.reshape(b, s, LANES), a2[l], gla_a_bias[l][None, :], gla_norm[l][None, :])
        x2 = _merge(x2, proj2, ya.reshape(b * s, V_W), yb.reshape(b * s, V_W), yc.reshape(b * s, V_W),
                    wa[l], wb[l], wc[l], wo[l])
        x2 = _ffn(x2, ffn2_norm[l][None, :], f2[0][l], f2[1][l], f2[2][l], fin, l == DEPTH - 1)
    return x2.reshape(b, s, d)
```
